```python
import jax, jax.numpy as jnp
from jax import lax
import numpy as np

D_MODEL = 2048
BATCH = 2
SEQ = 4096
DEPTH = 4
DEC_BATCH = 8
DEC_SEQ = 8
PAST_LEN = 16384
PAGE_SIZE = 128

N_EVEN = (DEPTH + 1) // 2
N_ODD = DEPTH // 2
EPS = 1e-6
NEG_BIG = -1e30

H_A = 8
DH_A = 128
W_A = H_A * DH_A
DILATED_GROUPS = ((128, 1), (512, 4), (2048, 16))
WINDOW_MAX = 2048
ATTN_BLOCK = 128

W_B = 1024
H_B = 8
BH_B = W_B // H_B
CONV_B = 4
LRU_C = 8.0

H_C = 8
DK_C = 128
DV_C = 128
KEY_C = H_C * DK_C
W_C = H_C * DV_C

H_D = 4
DK_D = 128
DV_D = 256
KEY_D = H_D * DK_D
W_D = H_D * DV_D
GATE_RANK = 16
GLA_TAU = 16.0
CHUNK = 64

D_FF = 5504
CONV_F = 3

IN_AB = 3 * W_A + 2 * W_B
MIX_AB = W_A + W_B
IN_CD = 2 * KEY_C + 2 * W_C + 2 * KEY_D + 2 * W_D + GATE_RANK
MIX_CD = W_C + W_D

kernel_name = 'hybrid_dilated_rglru_hgrn2_gla_step'

F32 = jnp.float32


def rmsnorm(x, g):
    xf = x.astype(F32)
    y = xf * lax.rsqrt(jnp.mean(xf * xf, axis=-1, keepdims=True) + EPS)
    return (y * g.astype(F32)).astype(x.dtype)


def split_cols(z, sizes):
    idx = np.cumsum(sizes)[:-1].tolist()
    return jnp.split(z, idx, axis=-1)


def causal_dwconv(x, buf, w, b):
    width = w.shape[0]
    T = x.shape[1]
    xp = jnp.concatenate([buf.astype(x.dtype), x], axis=1)
    y = b
    for j in range(width):
        y = y + w[j] * xp[:, j:j + T]
    return y, xp[:, xp.shape[1] - (width - 1):]


def dilated_attn_prompt(q, k, v, window, dilation):
    Bb, S, H, Dh = q.shape
    n = window // dilation
    QB = ATTN_BLOCK
    span = dilation * QB
    Sp = -(-S // span) * span
    NB = Sp // span

    def split(t):
        t = jnp.pad(t, ((0, 0), (0, Sp - S), (0, 0), (0, 0)))
        t = t.reshape(Bb, NB, QB, dilation, H, Dh)
        return jnp.transpose(t, (0, 3, 1, 2, 4, 5))

    def with_prev(t):
        prev = jnp.pad(t, ((0, 0), (0, 0), (1, 0), (0, 0), (0, 0), (0, 0)))[:, :, :-1]
        return jnp.concatenate([prev, t], axis=3)

    qb = split(q)
    kb = with_prev(split(k))
    vb = with_prev(split(v))
    s = jnp.einsum('brnihd,brnjhd->brnhij', qb, kb).astype(F32) * (Dh ** -0.5)
    nb = jnp.arange(NB)[:, None, None]
    ii = jnp.arange(QB)[None, :, None]
    jj = jnp.arange(2 * QB)[None, None, :]
    rel = ii - jj + QB
    valid = (rel >= 0) & (rel <= n) & (nb * QB + jj - QB >= 0)
    s = jnp.where(valid[None, None, :, None], s, NEG_BIG)
    m = jnp.max(s, axis=-1, keepdims=True)
    pr = jnp.where(valid[None, None, :, None], jnp.exp(s - m), 0.0)
    den = jnp.sum(pr, axis=-1)
    o = jnp.einsum('brnhij,brnjhd->brnihd', pr, vb.astype(F32)) / jnp.swapaxes(den, 3, 4)[..., None]
    lse = jnp.swapaxes(m[..., 0] + jnp.log(den), 3, 4)
    o = jnp.transpose(o, (0, 2, 3, 1, 4, 5)).reshape(Bb, Sp, H, Dh)[:, :S]
    lse = jnp.transpose(lse, (0, 2, 3, 1, 4)).reshape(Bb, Sp, H)[:, :S]
    return o, lse


def dilated_attn_step(q, kc, vc, window, dilation, L):
    T = q.shape[1]
    Dh = q.shape[-1]
    n = window // dilation
    idx = L + jnp.arange(T)[:, None] - dilation * jnp.arange(n + 1)[None, :]
    valid = idx >= 0
    idx = jnp.maximum(idx, 0)
    kg = jnp.take(kc, idx, axis=1)
    vg = jnp.take(vc, idx, axis=1)
    s = jnp.einsum('bthd,btjhd->bthj', q, kg).astype(F32) * (Dh ** -0.5)
    s = jnp.where(valid[None, :, None, :], s, NEG_BIG)
    m = jnp.max(s, axis=-1, keepdims=True)
    pr = jnp.where(valid[None, :, None, :], jnp.exp(s - m), 0.0)
    den = jnp.sum(pr, axis=-1)
    o = jnp.einsum('bthj,btjhd->bthd', pr, vg.astype(F32)) / den[..., None]
    return o, m[..., 0] + jnp.log(den)


def combine_dilations(res):
    outs = jnp.stack([o for o, _ in res])
    w = jax.nn.softmax(jnp.stack([l for _, l in res]), axis=0)
    return jnp.sum(w[..., None] * outs, axis=0)


def _lin_combine(left, right):
    a_l, u_l = left
    a_r, u_r = right
    return a_l * a_r, a_r * u_l + u_r


def rg_lru(x, h0, wa, ba, wx, bx, lam):
    Bb, T, _ = x.shape
    xf = x.astype(F32)
    xh = xf.reshape(Bb, T, H_B, BH_B)
    r = jax.nn.sigmoid(jnp.einsum('bthi,hij->bthj', xh, wa.astype(F32)).reshape(Bb, T, W_B) + ba)
    ig = jax.nn.sigmoid(jnp.einsum('bthi,hij->bthj', xh, wx.astype(F32)).reshape(Bb, T, W_B) + bx)
    log_a = -LRU_C * r * jax.nn.softplus(-lam.astype(F32))
    a = jnp.exp(log_a)
    u = jnp.sqrt(-jnp.expm1(2.0 * log_a)) * (ig * xf)
    u = u.at[:, 0].add(a[:, 0] * h0.astype(F32))
    _, h = lax.associative_scan(_lin_combine, (a, u), axis=1)
    return h, h[:, -1]


def chunk_gla(q, k, v, lg, s0):
    Bb, T, H, K = q.shape
    V = v.shape[-1]
    c = min(CHUNK, T)
    Tp = -(-T // c) * c

    def to_chunks(t):
        t = jnp.pad(t.astype(F32), ((0, 0), (0, Tp - T), (0, 0), (0, 0)))
        return jnp.moveaxis(t.reshape(Bb, Tp // c, c, H, t.shape[-1]), 1, 0)

    tri = jnp.tril(jnp.ones((c, c), bool))[None, :, :, None, None]

    def step(S, inp):
        qc, kc, vc, gc = inp
        b = jnp.cumsum(gc, axis=1)
        o_inter = jnp.einsum('bthk,bhkv->bthv', qc * jnp.exp(b), S)
        diff = b[:, :, None] - b[:, None, :]
        decay = jnp.where(tri, jnp.exp(jnp.where(tri, diff, 0.0)), 0.0)
        A = jnp.einsum('bthk,bshk,btshk->bhts', qc, kc, decay)
        o_intra = jnp.einsum('bhts,bshv->bthv', A, vc)
        b_last = b[:, -1]
        kdec = kc * jnp.exp(b_last[:, None] - b)
        S_new = jnp.exp(b_last)[..., None] * S + jnp.einsum('bshk,bshv->bhkv', kdec, vc)
        return S_new, o_inter + o_intra

    s_T, o = lax.scan(step, s0.astype(F32), (to_chunks(q), to_chunks(k), to_chunks(v), to_chunks(lg)))
    o = jnp.moveaxis(o, 0, 1).reshape(Bb, Tp, H, V)[:, :T]
    return o, s_T


def mix_ab(hn, p, i, kv_buf, conv_buf, h0):
    Bb, T, _ = hn.shape
    q, k, v, xb, gb = split_cols(hn @ p['w_in_ab'][i], (W_A, W_A, W_A, W_B, W_B))
    q = rmsnorm(q.reshape(Bb, T, H_A, DH_A), p['q_norm'][i])
    k = rmsnorm(k.reshape(Bb, T, H_A, DH_A), p['k_norm'][i])
    v = v.reshape(Bb, T, H_A, DH_A)
    if kv_buf is None:
        res = [dilated_attn_prompt(q, k, v, w, d) for (w, d) in DILATED_GROUPS]
        new_kv = jnp.stack([k, v], axis=2)[:, T - min(WINDOW_MAX, T):]
    else:
        L = kv_buf.shape[1]
        kc = jnp.concatenate([kv_buf[:, :, 0].astype(k.dtype), k], axis=1)
        vc = jnp.concatenate([kv_buf[:, :, 1].astype(v.dtype), v], axis=1)
        res = [dilated_attn_step(q, kc, vc, w, d, L) for (w, d) in DILATED_GROUPS]
        new_kv = jnp.stack([k, v], axis=2)
    o_a = combine_dilations(res).reshape(Bb, T, W_A)
    xc, new_conv = causal_dwconv(xb, conv_buf, p['lru_conv_w'][i], p['lru_conv_b'][i])
    hs, h_last = rg_lru(xc, h0, p['lru_wa'][i], p['lru_ba'][i], p['lru_wx'][i], p['lru_bx'][i], p['lru_lambda'][i])
    o_b = hs * jax.nn.gelu(gb.astype(F32))
    y = jnp.concatenate([o_a, o_b], axis=-1).astype(hn.dtype) @ p['w_out_ab'][i]
    return y, new_kv, new_conv, h_last


def mix_cd(hn, p, i, lb, s_c0, s_d0):
    Bb, T, _ = hn.shape
    qc, fc, ic, gc, qd, kd, vd, rd, ad = split_cols(
        hn @ p['w_in_cd'][i], (KEY_C, KEY_C, W_C, W_C, KEY_D, KEY_D, W_D, W_D, GATE_RANK))
    zf = fc.astype(F32).reshape(Bb, T, H_C, DK_C)
    lbh = lb.reshape(H_C, DK_C)
    log_f = jnp.log(lbh + (1.0 - lbh) * jax.nn.sigmoid(zf))
    k_c = (1.0 - lbh) * jax.nn.sigmoid(-zf)
    o_c, s_c = chunk_gla(qc.reshape(Bb, T, H_C, DK_C), k_c, ic.reshape(Bb, T, H_C, DV_C), log_f, s_c0)
    o_c = rmsnorm(o_c, p['hgrn_norm'][i]) * jax.nn.silu(gc.astype(F32).reshape(Bb, T, H_C, DV_C))
    z_a = ad.astype(F32) @ p['gla_wg2'][i].astype(F32) + p['gla_bg'][i]
    log_alpha = (jax.nn.log_sigmoid(z_a) / GLA_TAU).reshape(Bb, T, H_D, DK_D)
    o_d, s_d = chunk_gla(qd.reshape(Bb, T, H_D, DK_D) * (DK_D ** -0.5), kd.reshape(Bb, T, H_D, DK_D),
                         vd.reshape(Bb, T, H_D, DV_D), log_alpha, s_d0)
    o_d = rmsnorm(o_d, p['gla_norm'][i]) * jax.nn.silu(rd.astype(F32).reshape(Bb, T, H_D, DV_D))
    y = jnp.concatenate([o_c.reshape(Bb, T, W_C), o_d.reshape(Bb, T, W_D)], axis=-1).astype(hn.dtype) @ p['w_out_cd'][i]
    return y, s_c, s_d


def conv_ffn(hn, p, layer, buf):
    g, u = split_cols(hn @ p['ffn_w_up'][layer], (D_FF, D_FF))
    gc, new_buf = causal_dwconv(g, buf, p['ffn_conv_w'][layer], p['ffn_conv_b'][layer])
    return (jax.nn.silu(gc) * u) @ p['ffn_w_down'][layer], new_buf


def run_trunk(x, p, st):
    Bb = x.shape[0]
    dt = x.dtype
    pr = jax.nn.softmax(p['hgrn_lb_logits'].astype(F32), axis=0)
    lower_bounds = jnp.cumsum(pr, axis=0) - pr[0]
    kvs, convs, hs, shg, sgl, ffs = [], [], [], [], [], []
    for layer in range(DEPTH):
        i = layer // 2
        hn = rmsnorm(x, p['norm_mix'][layer])
        if layer % 2 == 0:
            if st is None:
                kv_buf = None
                conv_buf = jnp.zeros((Bb, CONV_B - 1, W_B), dt)
                h0 = jnp.zeros((Bb, W_B), F32)
            else:
                kv_buf, conv_buf, h0 = st[0][i], st[1][i], st[2][i]
            y, n_kv, n_conv, n_h = mix_ab(hn, p, i, kv_buf, conv_buf, h0)
            kvs.append(n_kv)
            convs.append(n_conv)
            hs.append(n_h)
        else:
            if st is None:
                s_c0 = jnp.zeros((Bb, H_C, DK_C, DV_C), F32)
                s_d0 = jnp.zeros((Bb, H_D, DK_D, DV_D), F32)
            else:
                s_c0, s_d0 = st[3][i], st[4][i]
            y, n_c, n_d = mix_cd(hn, p, i, lower_bounds[i], s_c0, s_d0)
            shg.append(n_c)
            sgl.append(n_d)
        x = x + y.astype(dt)
        hn = rmsnorm(x, p['norm_ffn'][layer])
        fbuf = jnp.zeros((Bb, CONV_F - 1, D_FF), dt) if st is None else st[5][layer]
        y, n_f = conv_ffn(hn, p, layer, fbuf)
        ffs.append(n_f)
        x = x + y.astype(dt)
    return x, jnp.stack(kvs), jnp.stack(convs), jnp.stack(hs), jnp.stack(shg), jnp.stack(sgl), jnp.stack(ffs)


def setup_inputs(seed: int = 0) -> dict:
    key = jax.random.key(seed)
    ks = iter(jax.random.split(key, 48))

    def nrm(shape, scale):
        return scale * jax.random.normal(next(ks), shape, F32)

    l_a = min(WINDOW_MAX, PAST_LEN)
    a_init = jax.random.uniform(next(ks), (N_EVEN, W_B), F32, 0.9, 0.999)
    s = a_init ** (1.0 / LRU_C)
    lru_lambda = jnp.log(s) - jnp.log1p(-s)
    return {
        'x_prompt': nrm((BATCH, SEQ, D_MODEL), 1.0),
        'x_sample': nrm((DEC_BATCH, DEC_SEQ, D_MODEL), 1.0),
        'cache_attn_kv': nrm((N_EVEN, DEC_BATCH, l_a, 2, H_A, DH_A), 1.0),
        'state_lru_conv': nrm((N_EVEN, DEC_BATCH, CONV_B - 1, W_B), 1.0),
        'state_lru_h': nrm((N_EVEN, DEC_BATCH, W_B), 0.5),
        'state_hgrn': nrm((N_ODD, DEC_BATCH, H_C, DK_C, DV_C), 1.0),
        'state_gla': nrm((N_ODD, DEC_BATCH, H_D, DK_D, DV_D), 1.0),
        'state_ffn_conv': nrm((DEPTH, DEC_BATCH, CONV_F - 1, D_FF), 1.0),
        'norm_mix': 1.0 + nrm((DEPTH, D_MODEL), 0.02),
        'norm_ffn': 1.0 + nrm((DEPTH, D_MODEL), 0.02),
        'w_in_ab': nrm((N_EVEN, D_MODEL, IN_AB), D_MODEL ** -0.5),
        'q_norm': 1.0 + nrm((N_EVEN, DH_A), 0.02),
        'k_norm': 1.0 + nrm((N_EVEN, DH_A), 0.02),
        'lru_conv_w': nrm((N_EVEN, CONV_B, W_B), CONV_B ** -0.5),
        'lru_conv_b': nrm((N_EVEN, W_B), 0.01),
        'lru_wa': nrm((N_EVEN, H_B, BH_B, BH_B), BH_B ** -0.5),
        'lru_ba': nrm((N_EVEN, W_B), 0.01),
        'lru_wx': nrm((N_EVEN, H_B, BH_B, BH_B), BH_B ** -0.5),
        'lru_bx': nrm((N_EVEN, W_B), 0.01),
        'lru_lambda': lru_lambda,
        'w_out_ab': nrm((N_EVEN, MIX_AB, D_MODEL), MIX_AB ** -0.5),
        'w_in_cd': nrm((N_ODD, D_MODEL, IN_CD), D_MODEL ** -0.5),
        'hgrn_lb_logits': nrm((N_ODD, KEY_C), 1.0),
        'hgrn_norm': 1.0 + nrm((N_ODD, DV_C), 0.02),
        'gla_wg2': nrm((N_ODD, GATE_RANK, KEY_D), GATE_RANK ** -0.5),
        'gla_bg': nrm((N_ODD, KEY_D), 0.01),
        'gla_norm': 1.0 + nrm((N_ODD, DV_D), 0.02),
        'w_out_cd': nrm((N_ODD, MIX_CD, D_MODEL), MIX_CD ** -0.5),
        'ffn_w_up': nrm((DEPTH, D_MODEL, 2 * D_FF), D_MODEL ** -0.5),
        'ffn_conv_w': nrm((DEPTH, CONV_F, D_FF), CONV_F ** -0.5),
        'ffn_conv_b': nrm((DEPTH, D_FF), 0.01),
        'ffn_w_down': nrm((DEPTH, D_FF, D_MODEL), D_FF ** -0.5),
    }


def reference(x_prompt, x_sample, cache_attn_kv, state_lru_conv, state_lru_h, state_hgrn, state_gla,
              state_ffn_conv, norm_mix, norm_ffn, w_in_ab, q_norm, k_norm, lru_conv_w, lru_conv_b, lru_wa,
              lru_ba, lru_wx, lru_bx, lru_lambda, w_out_ab, w_in_cd, hgrn_lb_logits, hgrn_norm, gla_wg2, gla_bg,
              gla_norm, w_out_cd, ffn_w_up, ffn_conv_w, ffn_conv_b, ffn_w_down):
    p = {
        'norm_mix': norm_mix, 'norm_ffn': norm_ffn, 'w_in_ab': w_in_ab, 'q_norm': q_norm, 'k_norm': k_norm,
        'lru_conv_w': lru_conv_w, 'lru_conv_b': lru_conv_b, 'lru_wa': lru_wa, 'lru_ba': lru_ba,
        'lru_wx': lru_wx, 'lru_bx': lru_bx, 'lru_lambda': lru_lambda, 'w_out_ab': w_out_ab,
        'w_in_cd': w_in_cd, 'hgrn_lb_logits': hgrn_lb_logits, 'hgrn_norm': hgrn_norm, 'gla_wg2': gla_wg2,
        'gla_bg': gla_bg, 'gla_norm': gla_norm, 'w_out_cd': w_out_cd, 'ffn_w_up': ffn_w_up,
        'ffn_conv_w': ffn_conv_w, 'ffn_conv_b': ffn_conv_b, 'ffn_w_down': ffn_w_down,
    }
    y_prompt, p_kv, p_conv, p_h, p_hgrn, p_gla, p_ffn = run_trunk(x_prompt, p, None)
    st = (cache_attn_kv, state_lru_conv, state_lru_h, state_hgrn, state_gla, state_ffn_conv)
    y_sample, s_kv, s_conv, s_h, s_hgrn, s_gla, s_ffn = run_trunk(x_sample, p, st)
    return (y_prompt, y_sample, p_kv, s_kv, p_conv, s_conv, p_h, s_h, p_hgrn, s_hgrn, p_gla, s_gla, p_ffn, s_ffn)
```

```python
import functools

import jax
import jax.numpy as jnp
from jax import lax
from jax.experimental import pallas as pl
from jax.experimental.pallas import tpu as pltpu

F32 = jnp.float32
BF16 = jnp.bfloat16
EPS = 1e-6
NEG_BIG = -1e30

LANES = 128
SUBLANES = 8
VMEM_LIMIT = 56 * 1024 * 1024

H_A, DH_A = 8, 128
W_A = H_A * DH_A
DILATED_GROUPS = ((128, 1), (512, 4), (2048, 16))
ATTN_BLOCK = 128
W_B, H_B = 1024, 8
BH_B = W_B // H_B
CONV_B = 4
LRU_C = 8.0
H_C, DK_C, DV_C = 8, 128, 128
H_D, DK_D, DV_D = 4, 128, 256
GATE_RANK = 16
GLA_TAU = 16.0
CHUNK = 64
SUB_CHUNK = 16
D_FF = 5504
D_FF_PAD = 5632
CONV_F = 3
CD_MAIN = 7168
CD_PAD = 7680


def _cparams(n_axes):
    return pltpu.CompilerParams(dimension_semantics=("arbitrary",) * n_axes,
                                vmem_limit_bytes=VMEM_LIMIT)


def _dot(a, b):
    return jnp.dot(a.astype(BF16), b.astype(BF16), preferred_element_type=F32)


def _dot_nt(a, b):
    return lax.dot_general(a.astype(BF16), b.astype(BF16), (((1,), (1,)), ((), ())),
                           preferred_element_type=F32)


def _dot_tn(a, b):
    return lax.dot_general(a.astype(BF16), b.astype(BF16), (((0,), (0,)), ((), ())),
                           preferred_element_type=F32)


def _rms(x, gain):
    ms = jnp.mean(x * x, axis=-1, keepdims=True)
    return (x * lax.rsqrt(ms + EPS)) * gain


def _softplus(x):
    return jnp.maximum(x, 0.0) + jnp.log1p(jnp.exp(-jnp.abs(x)))


def _silu(x):
    return x * jax.nn.sigmoid(x)


def _gelu_tanh(x):
    return 0.5 * x * (1.0 + jnp.tanh(0.7978845608028654 * (x + 0.044715 * (x * x * x))))


def _norm_proj_kernel(x_ref, g_ref, w_ref, hg_ref, o_ref, xn_ref, *, n_head_tiles, tn):
    j = pl.program_id(1)

    @pl.when(j == 0)
    def _():
        xn_ref[...] = _rms(x_ref[...], g_ref[...]).astype(BF16)

    acc = jnp.dot(xn_ref[...], w_ref[...], preferred_element_type=F32)
    if n_head_tiles == 0:
        o_ref[...] = acc
        return

    @pl.when(j >= 2 * n_head_tiles)
    def _():
        o_ref[...] = acc

    @pl.when(j < 2 * n_head_tiles)
    def _():
        gain = hg_ref[pl.ds(lax.div(j, n_head_tiles), 1), :]
        for h in range(tn // LANES):
            sl = slice(h * LANES, (h + 1) * LANES)
            o_ref[:, sl] = _rms(acc[:, sl], gain)


def _norm_proj(x, gain, w, head_gain, *, tm, tn, n_head_tiles):
    m, d = x.shape
    n = w.shape[1]
    kern = functools.partial(_norm_proj_kernel, n_head_tiles=n_head_tiles, tn=tn)
    return pl.pallas_call(
        kern,
        grid=(m // tm, n // tn),
        in_specs=[
            pl.BlockSpec((tm, d), lambda i, j: (i, 0)),
            pl.BlockSpec((1, d), lambda i, j: (0, 0)),
            pl.BlockSpec((d, tn), lambda i, j: (0, j)),
            pl.BlockSpec(head_gain.shape, lambda i, j: (0, 0)),
        ],
        out_specs=pl.BlockSpec((tm, tn), lambda i, j: (i, j)),
        out_shape=jax.ShapeDtypeStruct((m, n), F32),
        scratch_shapes=[pltpu.VMEM((tm, d), BF16)],
        compiler_params=_cparams(2),
        name="norm_proj",
    )(x, gain, w, head_gain)


def _attn_prompt_kernel(q_ref, kp_ref, kc_ref, vp_ref, vc_ref, o_ref, l_ref, *, n_keys):
    nb = pl.program_id(2)
    qb = ATTN_BLOCK
    row = lax.broadcasted_iota(jnp.int32, (qb, 2 * qb), 0)
    col = lax.broadcasted_iota(jnp.int32, (qb, 2 * qb), 1)
    valid = (col >= row + (qb - n_keys)) & (col <= row + qb) & ((nb > 0) | (col >= qb))
    scale = DH_A ** -0.5
    for h in range(H_A):
        sl = slice(h * DH_A, (h + 1) * DH_A)
        q = q_ref[0, :, sl]
        k = jnp.concatenate([kp_ref[0, :, sl], kc_ref[0, :, sl]], axis=0)
        v = jnp.concatenate([vp_ref[0, :, sl], vc_ref[0, :, sl]], axis=0)
        s = _dot_nt(q, k) * scale
        s = jnp.where(valid, s, NEG_BIG)
        m = jnp.max(s, axis=-1, keepdims=True)
        p = jnp.where(valid, jnp.exp(s - m), 0.0)
        den = jnp.sum(p, axis=-1, keepdims=True)
        o_ref[0, :, sl] = _dot(p, v) / den
        l_ref[0, :, sl] = jnp.broadcast_to(m + jnp.log(den), (qb, DH_A))


def _attn_prompt_group(z, window, dilation):
    bsz, s, zc = z.shape
    d = dilation
    n_keys = window // d
    assert n_keys <= ATTN_BLOCK and s % (d * ATTN_BLOCK) == 0
    nblk = zc // W_A
    zr = z.reshape(bsz, s // d, d * zc)
    nb = s // (d * ATTN_BLOCK)
    blk = (1, ATTN_BLOCK, W_A)

    def cur(c):
        return pl.BlockSpec(blk, lambda b, r, n: (b, n, r * nblk + c))

    def prev(c):
        return pl.BlockSpec(blk, lambda b, r, n: (b, jnp.maximum(n - 1, 0), r * nblk + c))

    out_spec = pl.BlockSpec(blk, lambda b, r, n: (b, n, r))
    o, lse = pl.pallas_call(
        functools.partial(_attn_prompt_kernel, n_keys=n_keys),
        grid=(bsz, d, nb),
        in_specs=[cur(0), prev(1), cur(1), prev(2), cur(2)],
        out_specs=[out_spec, out_spec],
        out_shape=[jax.ShapeDtypeStruct((bsz, s // d, d * W_A), F32)] * 2,
        compiler_params=_cparams(3),
        name="attn_prompt",
    )(zr, zr, zr, zr, zr)
    return o.reshape(bsz * s, W_A), lse.reshape(bsz * s, W_A)


def _combine_kernel(o1, l1, o2, l2, o3, l3, out_ref):
    la, lb, lc = l1[...], l2[...], l3[...]
    m = jnp.maximum(jnp.maximum(la, lb), lc)
    ea, eb, ec = jnp.exp(la - m), jnp.exp(lb - m), jnp.exp(lc - m)
    den = ea + eb + ec
    out_ref[...] = ((ea / den) * o1[...] + (eb / den) * o2[...] + (ec / den) * o3[...]).astype(out_ref.dtype)


def _combine(parts, *, tm):
    m = parts[0].shape[0]
    spec = pl.BlockSpec((tm, W_A), lambda i: (i, 0))
    return pl.pallas_call(
        _combine_kernel,
        grid=(m // tm,),
        in_specs=[spec] * 6,
        out_specs=spec,
        out_shape=jax.ShapeDtypeStruct((m, W_A), BF16),
        compiler_params=_cparams(1),
        name="attn_combine",
    )(*parts)


def _attn_step_kernel(q_ref, kn_ref, vn_ref, kc_ref, vc_ref, o_ref, *, cache_len, t_new):
    scale = DH_A ** -0.5
    q = q_ref[0]
    kc, vc = kc_ref[0], vc_ref[0]
    kn, vn = kn_ref[0], vn_ref[0]
    sc = _dot_nt(q, kc) * scale
    sn = _dot_nt(q, kn) * scale
    dc = (cache_len + lax.broadcasted_iota(jnp.int32, (t_new, cache_len), 0)
          - lax.broadcasted_iota(jnp.int32, (t_new, cache_len), 1))
    dn = (lax.broadcasted_iota(jnp.int32, (t_new, t_new), 0)
          - lax.broadcasted_iota(jnp.int32, (t_new, t_new), 1))
    outs, lses = [], []
    for window, d in DILATED_GROUPS:
        assert d & (d - 1) == 0
        okc = (dc <= window) & ((dc & (d - 1)) == 0)
        okn = (dn >= 0) & (dn <= window) & ((dn & (d - 1)) == 0)
        m = jnp.maximum(jnp.max(jnp.where(okc, sc, NEG_BIG), axis=-1, keepdims=True),
                        jnp.max(jnp.where(okn, sn, NEG_BIG), axis=-1, keepdims=True))
        pc = jnp.where(okc, jnp.exp(sc - m), 0.0)
        pn = jnp.where(okn, jnp.exp(sn - m), 0.0)
        den = jnp.sum(pc, axis=-1, keepdims=True) + jnp.sum(pn, axis=-1, keepdims=True)
        outs.append((_dot(pc, vc) + _dot(pn, vn)) / den)
        lses.append(m + jnp.log(den))
    m = jnp.maximum(jnp.maximum(lses[0], lses[1]), lses[2])
    es = [jnp.exp(l - m) for l in lses]
    den = es[0] + es[1] + es[2]
    o_ref[0] = (es[0] / den) * outs[0] + (es[1] / den) * outs[1] + (es[2] / den) * outs[2]


def _attn_step(z, cache):
    bsz, t_new, _ = z.shape
    cache_len = cache.shape[1]
    new = (1, t_new, DH_A)
    old = (1, cache_len, DH_A)
    return pl.pallas_call(
        functools.partial(_attn_step_kernel, cache_len=cache_len, t_new=t_new),
        grid=(bsz, H_A),
        in_specs=[
            pl.BlockSpec(new, lambda b, h: (b, 0, h)),
            pl.BlockSpec(new, lambda b, h: (b, 0, H_A + h)),
            pl.BlockSpec(new, lambda b, h: (b, 0, 2 * H_A + h)),
            pl.BlockSpec(old, lambda b, h: (b, 0, h)),
            pl.BlockSpec(old, lambda b, h: (b, 0, H_A + h)),
        ],
        out_specs=pl.BlockSpec(new, lambda b, h: (b, 0, h)),
        out_shape=jax.ShapeDtypeStruct((bsz, t_new, W_A), F32),
        compiler_params=_cparams(2),
        name="attn_step",
    )(z, z, z, cache, cache)


def _rglru_kernel(xb_ref, gb_ref, cbuf_ref, h0_ref, cw_ref, cb_ref, wa_ref, ba_ref, wx_ref, bx_ref,
                  lam_ref, ob_ref, nconv_ref, hlast_ref, xp_ref, a_ref, u_ref, hs_ref, hc_ref, *, tt):
    i = pl.program_id(1)
    halo = SUBLANES

    @pl.when(i == 0)
    def _():
        xp_ref[0:halo, :] = cbuf_ref[0]
        hc_ref[...] = h0_ref[0]

    @pl.when(i > 0)
    def _():
        xp_ref[0:halo, :] = xp_ref[tt:tt + halo, :]

    xp_ref[halo:halo + tt, :] = xb_ref[0]
    nconv_ref[0] = xp_ref[tt:tt + halo, :]

    xc = cb_ref[...]
    for j in range(CONV_B):
        off = halo - (CONV_B - 1) + j
        xc = xc + cw_ref[j:j + 1, :] * xp_ref[off:off + tt, :]

    for h in range(H_B):
        sl = slice(h * BH_B, (h + 1) * BH_B)
        xh = xc[:, sl]
        r = jax.nn.sigmoid(_dot(xh, wa_ref[h]) + ba_ref[:, sl])
        ig = jax.nn.sigmoid(_dot(xh, wx_ref[h]) + bx_ref[:, sl])
        log_a = -LRU_C * r * _softplus(-lam_ref[:, sl])
        a_ref[:, sl] = jnp.exp(log_a)
        one_minus_a2 = -jnp.tanh(log_a) * (jnp.exp(2.0 * log_a) + 1.0)
        u_ref[:, sl] = jnp.sqrt(one_minus_a2) * (ig * xh)

    def block(kb, h):
        base = pl.multiple_of(kb * SUBLANES, SUBLANES)
        a8 = a_ref[pl.ds(base, SUBLANES), :]
        u8 = u_ref[pl.ds(base, SUBLANES), :]
        rows = []
        for r in range(SUBLANES):
            h = a8[r:r + 1] * h + u8[r:r + 1]
            rows.append(h)
        hs_ref[pl.ds(base, SUBLANES), :] = jnp.concatenate(rows, axis=0)
        return h

    h = lax.fori_loop(0, tt // SUBLANES, block, hc_ref[...])
    hc_ref[...] = h
    hlast_ref[0] = h
    ob_ref[0] = (hs_ref[...] * _gelu_tanh(gb_ref[0])).astype(ob_ref.dtype)


def _rglru(z, conv_buf8, h0, p, *, tt):
    bsz, s, _ = z.shape
    tile = pl.BlockSpec((1, tt, W_B), lambda b, i: (b, i, 3))
    gate = pl.BlockSpec((1, tt, W_B), lambda b, i: (b, i, 4))
    per_b8 = pl.BlockSpec((1, SUBLANES, W_B), lambda b, i: (b, 0, 0))
    per_b1 = pl.BlockSpec((1, 1, W_B), lambda b, i: (b, 0, 0))

    def whole(a):
        return pl.BlockSpec(a.shape, lambda b, i: (0,) * a.ndim)

    weights = [p['cw'], p['cb'], p['wa'], p['ba'], p['wx'], p['bx'], p['lam']]
    return pl.pallas_call(
        functools.partial(_rglru_kernel, tt=tt),
        grid=(bsz, s // tt),
        in_specs=[tile, gate, per_b8, per_b1] + [whole(a) for a in weights],
        out_specs=[pl.BlockSpec((1, tt, W_B), lambda b, i: (b, i, 0)), per_b8, per_b1],
        out_shape=[jax.ShapeDtypeStruct((bsz, s, W_B), BF16),
                   jax.ShapeDtypeStruct((bsz, SUBLANES, W_B), F32),
                   jax.ShapeDtypeStruct((bsz, 1, W_B), F32)],
        scratch_shapes=[pltpu.VMEM((tt + SUBLANES, W_B), F32), pltpu.VMEM((tt, W_B), F32),
                        pltpu.VMEM((tt, W_B), F32), pltpu.VMEM((tt, W_B), F32),
                        pltpu.VMEM((1, W_B), F32)],
        compiler_params=_cparams(2),
        name="rglru",
    )(z, z, conv_buf8, h0, *weights)


def _gla_chunk(q, k, v, g, st, tri, *, c, sub):
    b = lax.dot_general(tri, g, (((1,), (0,)), ((), ())), precision=lax.Precision.HIGHEST,
                        preferred_element_type=F32)
    o = _dot_nt(q * jnp.exp(b), st)
    lane = lax.broadcasted_iota(jnp.int32, (sub, c), 1)
    rowi = lax.broadcasted_iota(jnp.int32, (sub, c), 0)
    rows = []
    for i in range(c // sub):
        bi = b[i * sub:(i + 1) * sub]
        qi = q[i * sub:(i + 1) * sub]
        ki = k[i * sub:(i + 1) * sub]
        if i == 0:
            arow = jnp.zeros((sub, c), F32)
        else:
            bs = b[i * sub - 1:i * sub]
            qs = qi * jnp.exp(bi - bs)
            ks = k * jnp.exp(jnp.minimum(bs - b, 0.0))
            arow = jnp.where(lane < i * sub, _dot_nt(qs, ks), 0.0)
        for s in range(sub):
            e = qi * (ki[s:s + 1] * jnp.exp(jnp.minimum(bi - bi[s:s + 1], 0.0)))
            col = jnp.sum(e, axis=-1, keepdims=True)
            arow = jnp.where((lane == i * sub + s) & (rowi >= s), col, arow)
        rows.append(arow)
    amat = rows[0] if len(rows) == 1 else jnp.concatenate(rows, axis=0)
    o = o + _dot(amat, v)
    blast = b[c - 1:c]
    kd = k * jnp.exp(blast - b)
    st_new = st * jnp.exp(blast) + _dot_tn(v, kd)
    return o, st_new


def _hgrn_kernel(z_ref, lbl_ref, s0_ref, ng_ref, o_ref, sout_ref, st_ref, *, layer_i, c, sub, tt):
    i = pl.program_id(1)

    @pl.when(i == 0)
    def _():
        for h in range(H_C):
            st_ref[h] = s0_ref[0, h].T

    lg = lbl_ref[...]
    e = jnp.exp(lg - jnp.max(lg, axis=0, keepdims=True))
    pr = e / jnp.sum(e, axis=0, keepdims=True)
    acc = pr[0:1]
    for l in range(1, layer_i + 1):
        acc = acc + pr[l:l + 1]
    lb = acc - pr[0:1]
    tri = (lax.broadcasted_iota(jnp.int32, (c, c), 0) >= lax.broadcasted_iota(jnp.int32, (c, c), 1)).astype(F32)
    ng = ng_ref[...]
    key_w = H_C * DK_C

    def chunk(ci, carry):
        r0 = pl.multiple_of(ci * c, c)
        rs = pl.ds(r0, c)
        for h in range(H_C):
            ks = slice(h * DK_C, (h + 1) * DK_C)
            q = z_ref[0, rs, ks]
            zf = z_ref[0, rs, key_w + h * DK_C:key_w + (h + 1) * DK_C]
            v = z_ref[0, rs, 2 * key_w + h * DV_C:2 * key_w + (h + 1) * DV_C]
            gate = z_ref[0, rs, 2 * key_w + H_C * DV_C + h * DV_C:2 * key_w + H_C * DV_C + (h + 1) * DV_C]
            lbh = lb[:, ks]
            g = jnp.log(lbh + (1.0 - lbh) * jax.nn.sigmoid(zf))
            k = (1.0 - lbh) * jax.nn.sigmoid(-zf)
            o, st_new = _gla_chunk(q, k, v, g, st_ref[h], tri, c=c, sub=sub)
            st_ref[h] = st_new
            o_ref[0, rs, h * DV_C:(h + 1) * DV_C] = (_rms(o, ng) * _silu(gate)).astype(o_ref.dtype)
        return carry

    lax.fori_loop(0, tt // c, chunk, 0)

    @pl.when(i == pl.num_programs(1) - 1)
    def _():
        for h in range(H_C):
            sout_ref[0, h] = st_ref[h].T


def _hgrn(z, lb_logits, s0, norm_gain, *, layer_i, tt, c, sub):
    bsz, s, _ = z.shape
    state = pl.BlockSpec((1, H_C, DK_C, DV_C), lambda b, i: (b, 0, 0, 0))
    width = 2 * H_C * DK_C + 2 * H_C * DV_C
    return pl.pallas_call(
        functools.partial(_hgrn_kernel, layer_i=layer_i, c=c, sub=sub, tt=tt),
        grid=(bsz, s // tt),
        in_specs=[pl.BlockSpec((1, tt, width), lambda b, i: (b, i, 0)),
                  pl.BlockSpec(lb_logits.shape, lambda b, i: (0, 0)),
                  state,
                  pl.BlockSpec((1, DV_C), lambda b, i: (0, 0))],
        out_specs=[pl.BlockSpec((1, tt, H_C * DV_C), lambda b, i: (b, i, 0)), state],
        out_shape=[jax.ShapeDtypeStruct((bsz, s, H_C * DV_C), BF16),
                   jax.ShapeDtypeStruct(s0.shape, F32)],
        scratch_shapes=[pltpu.VMEM((H_C, DV_C, DK_C), F32)],
        compiler_params=_cparams(2),
        name="hgrn2",
    )(z, lb_logits, s0, norm_gain)


def _gla_kernel(qk_ref, v_ref, r_ref, ad_ref, wg_ref, bg_ref, s0_ref, ng_ref, o_ref, sout_ref, st_ref,
                *, c, sub, tt):
    i = pl.program_id(1)

    @pl.when(i == 0)
    def _():
        for h in range(H_D):
            st_ref[h] = s0_ref[0, h].T

    tri = (lax.broadcasted_iota(jnp.int32, (c, c), 0) >= lax.broadcasted_iota(jnp.int32, (c, c), 1)).astype(F32)
    ng = ng_ref[...]
    key_w = H_D * DK_D
    scale = DK_D ** -0.5

    def chunk(ci, carry):
        r0 = pl.multiple_of(ci * c, c)
        rs = pl.ds(r0, c)
        za = _dot(ad_ref[0, rs, 0:LANES], wg_ref[...]) + bg_ref[...]
        lg = -_softplus(-za) * (1.0 / GLA_TAU)
        for h in range(H_D):
            ks = slice(h * DK_D, (h + 1) * DK_D)
            vs = slice(h * DV_D, (h + 1) * DV_D)
            q = qk_ref[0, rs, ks] * scale
            k = qk_ref[0, rs, key_w + h * DK_D:key_w + (h + 1) * DK_D]
            v = v_ref[0, rs, vs]
            o, st_new = _gla_chunk(q, k, v, lg[:, ks], st_ref[h], tri, c=c, sub=sub)
            st_ref[h] = st_new
            o_ref[0, rs, vs] = (_rms(o, ng) * _silu(r_ref[0, rs, vs])).astype(o_ref.dtype)
        return carry

    lax.fori_loop(0, tt // c, chunk, 0)

    @pl.when(i == pl.num_programs(1) - 1)
    def _():
        for h in range(H_D):
            sout_ref[0, h] = st_ref[h].T


def _gla(z, wg2p, bg, s0, norm_gain, *, tt, c, sub):
    bsz, s, _ = z.shape
    w = H_D * DV_D
    state = pl.BlockSpec((1, H_D, DK_D, DV_D), lambda b, i: (b, 0, 0, 0))

    def col(cb):
        return pl.BlockSpec((1, tt, w), lambda b, i: (b, i, cb))

    return pl.pallas_call(
        functools.partial(_gla_kernel, c=c, sub=sub, tt=tt),
        grid=(bsz, s // tt),
        in_specs=[col(4), col(5), col(6),
                  pl.BlockSpec((1, tt, CD_PAD - CD_MAIN), lambda b, i: (b, i, CD_MAIN // (CD_PAD - CD_MAIN))),
                  pl.BlockSpec(wg2p.shape, lambda b, i: (0, 0)),
                  pl.BlockSpec(bg.shape, lambda b, i: (0, 0)),
                  state,
                  pl.BlockSpec((1, DV_D), lambda b, i: (0, 0))],
        out_specs=[pl.BlockSpec((1, tt, w), lambda b, i: (b, i, 0)), state],
        out_shape=[jax.ShapeDtypeStruct((bsz, s, w), BF16),
                   jax.ShapeDtypeStruct(s0.shape, F32)],
        scratch_shapes=[pltpu.VMEM((H_D, DV_D, DK_D), F32)],
        compiler_params=_cparams(2),
        name="gla",
    )(z, z, z, z, wg2p, bg, s0, norm_gain)


def _out_proj_kernel(a1_ref, a2_ref, w1_ref, w2_ref, x_ref, o_ref):
    acc = _dot(a1_ref[...], w1_ref[...]) + _dot(a2_ref[...], w2_ref[...])
    o_ref[...] = x_ref[...] + acc


def _out_proj(a1, a2, w, x, *, tm, tn):
    m, k1 = a1.shape
    k2 = a2.shape[1]
    assert k1 == k2 and w.shape[0] == k1 + k2
    n = w.shape[1]
    return pl.pallas_call(
        _out_proj_kernel,
        grid=(m // tm, n // tn),
        in_specs=[pl.BlockSpec((tm, k1), lambda i, j: (i, 0)),
                  pl.BlockSpec((tm, k2), lambda i, j: (i, 0)),
                  pl.BlockSpec((k1, tn), lambda i, j: (0, j)),
                  pl.BlockSpec((k2, tn), lambda i, j: (1, j)),
                  pl.BlockSpec((tm, tn), lambda i, j: (i, j))],
        out_specs=pl.BlockSpec((tm, tn), lambda i, j: (i, j)),
        out_shape=jax.ShapeDtypeStruct((m, n), F32),
        compiler_params=_cparams(2),
        name="out_proj",
    )(a1, a2, w, w, x)


def _ffn_kernel(x_ref, gain_ref, wg_ref, wu_ref, cw_ref, cb_ref, wd_ref, buf_ref, o_ref, nbuf_ref,
                xn_ref, acc_ref, gp_ref, carry_ref, *, nseg, seg, tiles_per_seq):
    i = pl.program_id(0)
    j = pl.program_id(1)
    halo = SUBLANES

    @pl.when(j == 0)
    def _():
        xn_ref[...] = _rms(x_ref[...], gain_ref[...]).astype(BF16)
        acc_ref[...] = jnp.zeros_like(acc_ref)

    xn = xn_ref[...]
    g = jnp.dot(xn, wg_ref[...], preferred_element_type=F32)
    u = jnp.dot(xn, wu_ref[...], preferred_element_type=F32)
    first = lax.rem(i, tiles_per_seq) == 0

    @pl.when(first)
    def _():
        for s in range(nseg):
            gp_ref[s, 0:halo, :] = buf_ref[s]

    @pl.when(jnp.logical_not(first))
    def _():
        for s in range(nseg):
            gp_ref[s, 0:halo, :] = carry_ref[j, s]

    for s in range(nseg):
        gs = g[s * seg:(s + 1) * seg]
        gp_ref[s, halo:halo + seg, :] = gs
        carry_ref[j, s] = gs[seg - halo:seg]
        nbuf_ref[s] = gs[seg - halo:seg]

    for s in range(nseg):
        gc = cb_ref[...]
        for t in range(CONV_F):
            off = halo - (CONV_F - 1) + t
            gc = gc + cw_ref[t:t + 1, :] * gp_ref[s, off:off + seg, :]
        act = _silu(gc) * u[s * seg:(s + 1) * seg]
        acc_ref[s * seg:(s + 1) * seg, :] += _dot(act, wd_ref[...])

    @pl.when(j == pl.num_programs(1) - 1)
    def _():
        o_ref[...] = x_ref[...] + acc_ref[...]


def _ffn(x, gain, p, buf8, *, tm, tf, nseg, tiles_per_seq):
    m, d = x.shape
    seg = tm // nseg
    nj = D_FF_PAD // tf
    kern = functools.partial(_ffn_kernel, nseg=nseg, seg=seg, tiles_per_seq=tiles_per_seq)
    state = pl.BlockSpec((nseg, SUBLANES, tf), lambda i, j: (lax.div(i, tiles_per_seq), 0, j))
    tail = pl.BlockSpec((nseg, SUBLANES, tf), lambda i, j: (i, 0, j))
    n_tiles = m // tm
    tails = pl.pallas_call(
        kern,
        grid=(n_tiles, nj),
        in_specs=[pl.BlockSpec((tm, d), lambda i, j: (i, 0)),
                  pl.BlockSpec((1, d), lambda i, j: (0, 0)),
                  pl.BlockSpec((d, tf), lambda i, j: (0, j)),
                  pl.BlockSpec((d, tf), lambda i, j: (0, j)),
                  pl.BlockSpec((SUBLANES, tf), lambda i, j: (0, j)),
                  pl.BlockSpec((1, tf), lambda i, j: (0, j)),
                  pl.BlockSpec((tf, d), lambda i, j: (j, 0)),
                  state],
        out_specs=[pl.BlockSpec((tm, d), lambda i, j: (i, 0)), tail],
        out_shape=[jax.ShapeDtypeStruct((m, d), F32),
                   jax.ShapeDtypeStruct((n_tiles * nseg, SUBLANES, D_FF_PAD), F32)],
        scratch_shapes=[pltpu.VMEM((tm, d), BF16), pltpu.VMEM((tm, d), F32),
                        pltpu.VMEM((nseg, seg + SUBLANES, tf), F32),
                        pltpu.VMEM((nj, nseg, SUBLANES, tf), F32)],
        compiler_params=_cparams(2),
        name="conv_ffn",
    )(x, gain, p['wg'], p['wu'], p['cw'], p['cb'], p['wd'], buf8)
    y, nbuf = tails
    nbuf = nbuf.reshape(n_tiles // tiles_per_seq, tiles_per_seq, nseg, SUBLANES, D_FF_PAD)[:, -1]
    return y, nbuf.reshape(-1, SUBLANES, D_FF_PAD)


def _pad_to(a, axis, size):
    pad = [(0, 0)] * a.ndim
    pad[axis] = (0, size - a.shape[axis])
    return jnp.pad(a, pad)


def _tail_rows8(buf):
    return _pad_to(buf[:, ::-1], 1, SUBLANES)[:, ::-1]


def _stage_params(p):
    depth = p['norm_mix'].shape[0]
    layers = []
    for layer in range(depth):
        i = layer // 2
        up = p['ffn_w_up'][layer]
        lp = {
            'norm_mix': p['norm_mix'][layer][None],
            'norm_ffn': p['norm_ffn'][layer][None],
            'ffn': {
                'wg': _pad_to(up[:, :D_FF], 1, D_FF_PAD).astype(BF16),
                'wu': _pad_to(up[:, D_FF:], 1, D_FF_PAD).astype(BF16),
                'cw': _pad_to(_pad_to(p['ffn_conv_w'][layer], 1, D_FF_PAD), 0, SUBLANES),
                'cb': _pad_to(p['ffn_conv_b'][layer][None], 1, D_FF_PAD),
                'wd': _pad_to(p['ffn_w_down'][layer], 0, D_FF_PAD).astype(BF16),
            },
        }
        if layer % 2 == 0:
            lp.update({
                'w_in': p['w_in_ab'][i].astype(BF16),
                'head_gain': jnp.stack([p['q_norm'][i], p['k_norm'][i]]),
                'w_out': p['w_out_ab'][i].astype(BF16),
                'lru': {
                    'cw': _pad_to(p['lru_conv_w'][i], 0, SUBLANES),
                    'cb': p['lru_conv_b'][i][None],
                    'wa': p['lru_wa'][i].astype(BF16),
                    'ba': p['lru_ba'][i][None],
                    'wx': p['lru_wx'][i].astype(BF16),
                    'bx': p['lru_bx'][i][None],
                    'lam': p['lru_lambda'][i][None],
                },
            })
        else:
            lp.update({
                'w_in': _pad_to(p['w_in_cd'][i], 1, CD_PAD).astype(BF16),
                'w_out': p['w_out_cd'][i].astype(BF16),
                'hgrn_norm': p['hgrn_norm'][i][None],
                'gla_norm': p['gla_norm'][i][None],
                'wg2': _pad_to(p['gla_wg2'][i], 0, LANES).astype(BF16),
                'bg': p['gla_bg'][i][None],
            })
        layers.append(lp)
    return layers


def _run_trunk(x, layers, lb_logits, st, cfg):
    bsz, s, d = x.shape
    m = bsz * s
    xf = x.reshape(m, d)
    no_heads = jnp.zeros((2, LANES), F32)
    kvs, convs, hs, shg, sgl, ffs = [], [], [], [], [], []
    for layer, lp in enumerate(layers):
        i = layer // 2
        if layer % 2 == 0:
            z = _norm_proj(xf, lp['norm_mix'], lp['w_in'], lp['head_gain'], tm=cfg['tm'], tn=cfg['tn'],
                           n_head_tiles=W_A // cfg['tn'])
            z3 = z.reshape(bsz, s, -1)
            if st is None:
                parts = []
                for window, dil in DILATED_GROUPS:
                    parts.extend(_attn_prompt_group(z3, window, dil))
                o_a = _combine(parts, tm=256)
                keep = min(DILATED_GROUPS[-1][0], s)
                kvs.append(z3[:, s - keep:, W_A:3 * W_A].reshape(bsz, keep, 2, H_A, DH_A))
                conv_buf = jnp.zeros((bsz, CONV_B - 1, W_B), F32)
                h0 = jnp.zeros((bsz, W_B), F32)
            else:
                cache = st[0][i]
                o_a = _attn_step(z3, cache.reshape(bsz, cache.shape[1], 2 * W_A)).reshape(m, W_A)
                kvs.append(z3[:, :, W_A:3 * W_A].reshape(bsz, s, 2, H_A, DH_A))
                conv_buf, h0 = st[1][i], st[2][i]
            o_b, nconv, hlast = _rglru(z3, _tail_rows8(conv_buf), h0[:, None, :], lp['lru'], tt=cfg['tt'])
            convs.append(nconv[:, SUBLANES - (CONV_B - 1):])
            hs.append(hlast[:, 0])
            xf = _out_proj(o_a, o_b.reshape(m, W_B), lp['w_out'], xf, tm=cfg['tm'], tn=cfg['tn'])
        else:
            z = _norm_proj(xf, lp['norm_mix'], lp['w_in'], no_heads, tm=cfg['tm'], tn=cfg['tn'], n_head_tiles=0)
            z3 = z.reshape(bsz, s, -1)
            if st is None:
                s_c0 = jnp.zeros((bsz, H_C, DK_C, DV_C), F32)
                s_d0 = jnp.zeros((bsz, H_D, DK_D, DV_D), F32)
            else:
                s_c0, s_d0 = st[3][i], st[4][i]
            o_c, n_c = _hgrn(z3, lb_logits, s_c0, lp['hgrn_norm'], layer_i=i, tt=cfg['tg'],
                             c=cfg['chunk'], sub=cfg['sub'])
            o_d, n_d = _gla(z3, lp['wg2'], lp['bg'], s_d0, lp['gla_norm'], tt=cfg['tg'],
                            c=cfg['chunk'], sub=cfg['sub'])
            shg.append(n_c)
            sgl.append(n_d)
            xf = _out_proj(o_c.reshape(m, -1), o_d.reshape(m, -1), lp['w_out'], xf, tm=cfg['tm'], tn=cfg['tn'])
        fbuf = jnp.zeros((bsz, CONV_F - 1, D_FF), F32) if st is None else st[5][layer]
        buf8 = _pad_to(_tail_rows8(fbuf), 2, D_FF_PAD)
        xf, nbuf = _ffn(xf, lp['norm_ffn'], lp['ffn'], buf8, tm=cfg['ffn_tm'], tf=cfg['ffn_tf'],
                        nseg=cfg['ffn_nseg'], tiles_per_seq=cfg['ffn_tiles_per_seq'])
        ffs.append(nbuf[:, SUBLANES - (CONV_F - 1):, :D_FF])
    return (xf.reshape(bsz, s, d), jnp.stack(kvs), jnp.stack(convs), jnp.stack(hs), jnp.stack(shg),
            jnp.stack(sgl), jnp.stack(ffs))


def _config(bsz, s):
    m = bsz * s
    if s >= 512:
        return dict(tm=min(1024, m), tn=512, tt=512, tg=256, chunk=CHUNK, sub=SUB_CHUNK,
                    ffn_tm=512, ffn_tf=512, ffn_nseg=1, ffn_tiles_per_seq=s // 512)
    chunk = min(CHUNK, s)
    return dict(tm=m, tn=512, tt=s, tg=s, chunk=chunk, sub=min(SUB_CHUNK, chunk),
                ffn_tm=m, ffn_tf=512, ffn_nseg=bsz, ffn_tiles_per_seq=1)


def kernel(x_prompt, x_sample, cache_attn_kv, state_lru_conv, state_lru_h, state_hgrn, state_gla, state_ffn_conv, norm_mix, norm_ffn, w_in_ab, q_norm, k_norm, lru_conv_w, lru_conv_b, lru_wa, lru_ba, lru_wx, lru_bx, lru_lambda, w_out_ab, w_in_cd, hgrn_lb_logits, hgrn_norm, gla_wg2, gla_bg, gla_norm, w_out_cd, ffn_w_up, ffn_conv_w, ffn_conv_b, ffn_w_down):
    p = {
        'norm_mix': norm_mix, 'norm_ffn': norm_ffn, 'w_in_ab': w_in_ab, 'q_norm': q_norm, 'k_norm': k_norm,
        'lru_conv_w': lru_conv_w, 'lru_conv_b': lru_conv_b, 'lru_wa': lru_wa, 'lru_ba': lru_ba,
        'lru_wx': lru_wx, 'lru_bx': lru_bx, 'lru_lambda': lru_lambda, 'w_out_ab': w_out_ab,
        'w_in_cd': w_in_cd, 'hgrn_norm': hgrn_norm, 'gla_wg2': gla_wg2,
        'gla_bg': gla_bg, 'gla_norm': gla_norm, 'w_out_cd': w_out_cd, 'ffn_w_up': ffn_w_up,
        'ffn_conv_w': ffn_conv_w, 'ffn_conv_b': ffn_conv_b, 'ffn_w_down': ffn_w_down,
    }
    layers = _stage_params(p)
    y_p, p_kv, p_conv, p_h, p_hgrn, p_gla, p_ffn = _run_trunk(
        x_prompt, layers, hgrn_lb_logits, None, _config(*x_prompt.shape[:2]))
    st = (cache_attn_kv, state_lru_conv, state_lru_h, state_hgrn, state_gla, state_ffn_conv)
    y_s, s_kv, s_conv, s_h, s_hgrn, s_gla, s_ffn = _run_trunk(
        x_sample, layers, hgrn_lb_logits, st, _config(*x_sample.shape[:2]))
    return (y_p, y_s, p_kv, s_kv, p_conv, s_conv, p_h, s_h, p_hgrn, s_hgrn, p_gla, s_gla, p_ffn, s_ffn)
```

```python
import functools

import jax
import jax.numpy as jnp
from jax import lax
from jax.experimental import pallas as pl
from jax.experimental.pallas import tpu as pltpu

F32 = jnp.float32
BF16 = jnp.bfloat16
EPS = 1e-6
NEG_BIG = -1e30

LANES = 128
SUBLANES = 8
VMEM_LIMIT = 56 * 1024 * 1024

H_A, DH_A = 8, 128
W_A = H_A * DH_A
DILATED_GROUPS = ((128, 1), (512, 4), (2048, 16))
ATTN_BLOCK = 128
W_B, H_B = 1024, 8
BH_B = W_B // H_B
CONV_B = 4
LRU_C = 8.0
H_C, DK_C, DV_C = 8, 128, 128
H_D, DK_D, DV_D = 4, 128, 256
GATE_RANK = 16
GLA_TAU = 16.0
CHUNK = 64
SUB_CHUNK = 16
D_FF = 5504
D_FF_PAD = 5632
CONV_F = 3
CD_MAIN = 7168
CD_PAD = 7680


def _cparams(n_axes):
    return pltpu.CompilerParams(dimension_semantics=("arbitrary",) * n_axes,
                                vmem_limit_bytes=VMEM_LIMIT)


def _dot(a, b):
    return jnp.dot(a.astype(BF16), b.astype(BF16), preferred_element_type=F32)


def _dot_nt(a, b):
    return lax.dot_general(a.astype(BF16), b.astype(BF16), (((1,), (1,)), ((), ())),
                           preferred_element_type=F32)


def _dot_tn(a, b):
    return lax.dot_general(a.astype(BF16), b.astype(BF16), (((0,), (0,)), ((), ())),
                           preferred_element_type=F32)


def _rms(x, gain):
    ms = jnp.mean(x * x, axis=-1, keepdims=True)
    return (x * lax.rsqrt(ms + EPS)) * gain


def _softplus(x):
    return jnp.maximum(x, 0.0) + jnp.log1p(jnp.exp(-jnp.abs(x)))


def _silu(x):
    return x * jax.nn.sigmoid(x)


def _gelu_tanh(x):
    return 0.5 * x * (1.0 + jnp.tanh(0.7978845608028654 * (x + 0.044715 * (x * x * x))))


def _norm_proj_kernel(x_ref, g_ref, w_ref, hg_ref, o_ref, xn_ref, *, n_head_tiles, tn):
    j = pl.program_id(1)

    @pl.when(j == 0)
    def _():
        xn_ref[...] = _rms(x_ref[...], g_ref[...]).astype(BF16)

    acc = jnp.dot(xn_ref[...], w_ref[...], preferred_element_type=F32)
    if n_head_tiles == 0:
        o_ref[...] = acc
        return

    @pl.when(j >= 2 * n_head_tiles)
    def _():
        o_ref[...] = acc

    @pl.when(j < 2 * n_head_tiles)
    def _():
        gain = hg_ref[pl.ds(lax.div(j, n_head_tiles), 1), :]
        for h in range(tn // LANES):
            sl = slice(h * LANES, (h + 1) * LANES)
            o_ref[:, sl] = _rms(acc[:, sl], gain)


def _norm_proj(x, gain, w, head_gain, *, layer, tm, tn, n_head_tiles):
    m, d = x.shape
    n = w.shape[2]
    kern = functools.partial(_norm_proj_kernel, n_head_tiles=n_head_tiles, tn=tn)
    return pl.pallas_call(
        kern,
        grid=(m // tm, n // tn),
        in_specs=[
            pl.BlockSpec((tm, d), lambda i, j: (i, 0)),
            pl.BlockSpec((1, d), lambda i, j: (0, 0)),
            pl.BlockSpec((None, d, tn), lambda i, j: (layer, 0, j)),
            pl.BlockSpec(head_gain.shape, lambda i, j: (0, 0)),
        ],
        out_specs=pl.BlockSpec((tm, tn), lambda i, j: (i, j)),
        out_shape=jax.ShapeDtypeStruct((m, n), F32),
        scratch_shapes=[pltpu.VMEM((tm, d), BF16)],
        compiler_params=_cparams(2),
        name="norm_proj",
    )(x, gain, w, head_gain)


def _rows(ref, start, size, stride):
    return ref[0, pl.ds(start, size, stride=stride), :] if stride > 1 else ref[0, pl.ds(start, size), :]


def _attn_span_kernel(q_ref, kp_ref, kc_ref, vp_ref, vc_ref, o_ref, oacc_ref, lacc_ref, *, span):
    first_span = pl.program_id(1) == 0
    qb = ATTN_BLOCK
    row = lax.broadcasted_iota(jnp.int32, (qb, 2 * qb), 0)
    col = lax.broadcasted_iota(jnp.int32, (qb, 2 * qb), 1)
    scale = DH_A ** -0.5
    for g, (window, d) in enumerate(DILATED_GROUPS):
        n_keys = window // d
        band = (col >= row + (qb - n_keys)) & (col <= row + qb)
        band_first = band & (jnp.logical_not(first_span) | (col >= qb))
        blk = qb * d
        for sb in range(span // blk):
            valid = band_first if sb == 0 else band
            for r in range(d):
                q0 = sb * blk + r
                q = _rows(q_ref, q0, qb, d)
                if sb == 0:
                    k = jnp.concatenate([_rows(kp_ref, span - blk + r, qb, d), _rows(kc_ref, r, qb, d)], axis=0)
                    v = jnp.concatenate([_rows(vp_ref, span - blk + r, qb, d), _rows(vc_ref, r, qb, d)], axis=0)
                else:
                    k = _rows(kc_ref, q0 - blk, 2 * qb, d)
                    v = _rows(vc_ref, q0 - blk, 2 * qb, d)
                s = jnp.where(valid, _dot_nt(q, k) * scale, NEG_BIG)
                m = jnp.max(s, axis=-1, keepdims=True)
                p = jnp.where(valid, jnp.exp(s - m), 0.0)
                den = jnp.sum(p, axis=-1, keepdims=True)
                dst = pl.ds(q0, qb, stride=d) if d > 1 else pl.ds(q0, qb)
                oacc_ref[g, dst, :] = _dot(p, v) / den
                lacc_ref[g, dst, :] = jnp.broadcast_to(m + jnp.log(den), (qb, DH_A))
    la, lb, lc = lacc_ref[0], lacc_ref[1], lacc_ref[2]
    m = jnp.maximum(jnp.maximum(la, lb), lc)
    ea, eb, ec = jnp.exp(la - m), jnp.exp(lb - m), jnp.exp(lc - m)
    den = ea + eb + ec
    o_ref[0] = ((ea / den) * oacc_ref[0] + (eb / den) * oacc_ref[1] + (ec / den) * oacc_ref[2]).astype(o_ref.dtype)


def _attn_prompt(z):
    bsz, s, _ = z.shape
    span = max(d for _, d in DILATED_GROUPS) * ATTN_BLOCK
    assert s % span == 0 and len(DILATED_GROUPS) == 3
    assert all(w // d <= ATTN_BLOCK and span % (d * ATTN_BLOCK) == 0 for w, d in DILATED_GROUPS)
    blk = (1, span, DH_A)

    def cur(c):
        return pl.BlockSpec(blk, lambda b, n, h: (b, n, c * H_A + h))

    def prev(c):
        return pl.BlockSpec(blk, lambda b, n, h: (b, jnp.maximum(n - 1, 0), c * H_A + h))

    return pl.pallas_call(
        functools.partial(_attn_span_kernel, span=span),
        grid=(bsz, s // span, H_A),
        in_specs=[cur(0), prev(1), cur(1), prev(2), cur(2)],
        out_specs=pl.BlockSpec(blk, lambda b, n, h: (b, n, h)),
        out_shape=jax.ShapeDtypeStruct((bsz, s, W_A), BF16),
        scratch_shapes=[pltpu.VMEM((3, span, DH_A), F32), pltpu.VMEM((3, span, DH_A), F32)],
        compiler_params=_cparams(3),
        name="attn_prompt",
    )(z, z, z, z, z)


def _attn_step_kernel(q_ref, kn_ref, vn_ref, kc_ref, vc_ref, o_ref, *, cache_len, t_new):
    scale = DH_A ** -0.5
    q = q_ref[0]
    kc, vc = kc_ref[0], vc_ref[0]
    kn, vn = kn_ref[0], vn_ref[0]
    sc = _dot_nt(q, kc) * scale
    sn = _dot_nt(q, kn) * scale
    dc = (cache_len + lax.broadcasted_iota(jnp.int32, (t_new, cache_len), 0)
          - lax.broadcasted_iota(jnp.int32, (t_new, cache_len), 1))
    dn = (lax.broadcasted_iota(jnp.int32, (t_new, t_new), 0)
          - lax.broadcasted_iota(jnp.int32, (t_new, t_new), 1))
    outs, lses = [], []
    for window, d in DILATED_GROUPS:
        assert d & (d - 1) == 0
        okc = (dc <= window) & ((dc & (d - 1)) == 0)
        okn = (dn >= 0) & (dn <= window) & ((dn & (d - 1)) == 0)
        m = jnp.maximum(jnp.max(jnp.where(okc, sc, NEG_BIG), axis=-1, keepdims=True),
                        jnp.max(jnp.where(okn, sn, NEG_BIG), axis=-1, keepdims=True))
        pc = jnp.where(okc, jnp.exp(sc - m), 0.0)
        pn = jnp.where(okn, jnp.exp(sn - m), 0.0)
        den = jnp.sum(pc, axis=-1, keepdims=True) + jnp.sum(pn, axis=-1, keepdims=True)
        outs.append((_dot(pc, vc) + _dot(pn, vn)) / den)
        lses.append(m + jnp.log(den))
    m = jnp.maximum(jnp.maximum(lses[0], lses[1]), lses[2])
    es = [jnp.exp(l - m) for l in lses]
    den = es[0] + es[1] + es[2]
    o_ref[0] = (es[0] / den) * outs[0] + (es[1] / den) * outs[1] + (es[2] / den) * outs[2]


def _attn_step(z, cache, *, layer):
    bsz, t_new, _ = z.shape
    cache_len = cache.shape[2]
    new = (1, t_new, DH_A)
    old = (None, 1, cache_len, DH_A)
    return pl.pallas_call(
        functools.partial(_attn_step_kernel, cache_len=cache_len, t_new=t_new),
        grid=(bsz, H_A),
        in_specs=[
            pl.BlockSpec(new, lambda b, h: (b, 0, h)),
            pl.BlockSpec(new, lambda b, h: (b, 0, H_A + h)),
            pl.BlockSpec(new, lambda b, h: (b, 0, 2 * H_A + h)),
            pl.BlockSpec(old, lambda b, h: (layer, b, 0, h)),
            pl.BlockSpec(old, lambda b, h: (layer, b, 0, H_A + h)),
        ],
        out_specs=pl.BlockSpec(new, lambda b, h: (b, 0, h)),
        out_shape=jax.ShapeDtypeStruct((bsz, t_new, W_A), F32),
        compiler_params=_cparams(2),
        name="attn_step",
    )(z, z, z, cache, cache)


def _rglru_kernel(xb_ref, gb_ref, cbuf_ref, h0_ref, cw_ref, cb_ref, wa_ref, ba_ref, wx_ref, bx_ref,
                  lam_ref, ob_ref, nconv_ref, hlast_ref, xp_ref, a_ref, u_ref, hs_ref, hc_ref, *, tt):
    i = pl.program_id(1)
    halo = SUBLANES

    @pl.when(i == 0)
    def _():
        xp_ref[0:halo, :] = cbuf_ref[0]
        hc_ref[...] = h0_ref[0]

    @pl.when(i > 0)
    def _():
        xp_ref[0:halo, :] = xp_ref[tt:tt + halo, :]

    xp_ref[halo:halo + tt, :] = xb_ref[0]
    nconv_ref[0] = xp_ref[tt:tt + halo, :]

    xc = cb_ref[...]
    for j in range(CONV_B):
        off = halo - (CONV_B - 1) + j
        xc = xc + cw_ref[j:j + 1, :] * xp_ref[off:off + tt, :]

    for h in range(H_B):
        sl = slice(h * BH_B, (h + 1) * BH_B)
        xh = xc[:, sl]
        r = jax.nn.sigmoid(_dot(xh, wa_ref[h]) + ba_ref[:, sl])
        ig = jax.nn.sigmoid(_dot(xh, wx_ref[h]) + bx_ref[:, sl])
        log_a = -LRU_C * r * _softplus(-lam_ref[:, sl])
        a_ref[:, sl] = jnp.exp(log_a)
        one_minus_a2 = -jnp.tanh(log_a) * (jnp.exp(2.0 * log_a) + 1.0)
        u_ref[:, sl] = jnp.sqrt(one_minus_a2) * (ig * xh)

    def block(kb, h):
        base = pl.multiple_of(kb * SUBLANES, SUBLANES)
        a8 = a_ref[pl.ds(base, SUBLANES), :]
        u8 = u_ref[pl.ds(base, SUBLANES), :]
        rows = []
        for r in range(SUBLANES):
            h = a8[r:r + 1] * h + u8[r:r + 1]
            rows.append(h)
        hs_ref[pl.ds(base, SUBLANES), :] = jnp.concatenate(rows, axis=0)
        return h

    h = lax.fori_loop(0, tt // SUBLANES, block, hc_ref[...])
    hc_ref[...] = h
    hlast_ref[0] = h
    ob_ref[0] = (hs_ref[...] * _gelu_tanh(gb_ref[0])).astype(ob_ref.dtype)


def _rglru(z, conv_buf8, h0, p, *, tt):
    bsz, s, _ = z.shape
    tile = pl.BlockSpec((1, tt, W_B), lambda b, i: (b, i, 3))
    gate = pl.BlockSpec((1, tt, W_B), lambda b, i: (b, i, 4))
    per_b8 = pl.BlockSpec((1, SUBLANES, W_B), lambda b, i: (b, 0, 0))
    per_b1 = pl.BlockSpec((1, 1, W_B), lambda b, i: (b, 0, 0))

    def whole(a):
        return pl.BlockSpec(a.shape, lambda b, i: (0,) * a.ndim)

    weights = [p['cw'], p['cb'], p['wa'], p['ba'], p['wx'], p['bx'], p['lam']]
    return pl.pallas_call(
        functools.partial(_rglru_kernel, tt=tt),
        grid=(bsz, s // tt),
        in_specs=[tile, gate, per_b8, per_b1] + [whole(a) for a in weights],
        out_specs=[pl.BlockSpec((1, tt, W_B), lambda b, i: (b, i, 0)), per_b8, per_b1],
        out_shape=[jax.ShapeDtypeStruct((bsz, s, W_B), BF16),
                   jax.ShapeDtypeStruct((bsz, SUBLANES, W_B), F32),
                   jax.ShapeDtypeStruct((bsz, 1, W_B), F32)],
        scratch_shapes=[pltpu.VMEM((tt + SUBLANES, W_B), F32), pltpu.VMEM((tt, W_B), F32),
                        pltpu.VMEM((tt, W_B), F32), pltpu.VMEM((tt, W_B), F32),
                        pltpu.VMEM((1, W_B), F32)],
        compiler_params=_cparams(2),
        name="rglru",
    )(z, z, conv_buf8, h0, *weights)


def _gla_chunk(q, k, v, g, st, tri, *, c, sub):
    b = lax.dot_general(tri, g, (((1,), (0,)), ((), ())), precision=lax.Precision.HIGHEST,
                        preferred_element_type=F32)
    o = _dot_nt(q * jnp.exp(b), st)
    lane = lax.broadcasted_iota(jnp.int32, (sub, c), 1)
    rowi = lax.broadcasted_iota(jnp.int32, (sub, c), 0)
    rows = []
    for i in range(c // sub):
        bi = b[i * sub:(i + 1) * sub]
        qi = q[i * sub:(i + 1) * sub]
        ki = k[i * sub:(i + 1) * sub]
        if i == 0:
            arow = jnp.zeros((sub, c), F32)
        else:
            bs = b[i * sub - 1:i * sub]
            qs = qi * jnp.exp(bi - bs)
            ks = k * jnp.exp(jnp.minimum(bs - b, 0.0))
            arow = jnp.where(lane < i * sub, _dot_nt(qs, ks), 0.0)
        for s in range(sub):
            e = qi * (ki[s:s + 1] * jnp.exp(jnp.minimum(bi - bi[s:s + 1], 0.0)))
            col = jnp.sum(e, axis=-1, keepdims=True)
            arow = jnp.where((lane == i * sub + s) & (rowi >= s), col, arow)
        rows.append(arow)
    amat = rows[0] if len(rows) == 1 else jnp.concatenate(rows, axis=0)
    o = o + _dot(amat, v)
    blast = b[c - 1:c]
    kd = k * jnp.exp(blast - b)
    st_new = st * jnp.exp(blast) + _dot_tn(v, kd)
    return o, st_new


def _hgrn_kernel(z_ref, lbl_ref, s0_ref, ng_ref, o_ref, sout_ref, st_ref, *, layer_i, c, sub, tt):
    i = pl.program_id(1)

    @pl.when(i == 0)
    def _():
        for h in range(H_C):
            st_ref[h] = s0_ref[0, h].T

    lg = lbl_ref[...]
    e = jnp.exp(lg - jnp.max(lg, axis=0, keepdims=True))
    pr = e / jnp.sum(e, axis=0, keepdims=True)
    acc = pr[0:1]
    for l in range(1, layer_i + 1):
        acc = acc + pr[l:l + 1]
    lb = acc - pr[0:1]
    tri = (lax.broadcasted_iota(jnp.int32, (c, c), 0) >= lax.broadcasted_iota(jnp.int32, (c, c), 1)).astype(F32)
    ng = ng_ref[...]
    key_w = H_C * DK_C

    def chunk(ci, carry):
        r0 = pl.multiple_of(ci * c, c)
        rs = pl.ds(r0, c)
        for h in range(H_C):
            ks = slice(h * DK_C, (h + 1) * DK_C)
            q = z_ref[0, rs, ks]
            zf = z_ref[0, rs, key_w + h * DK_C:key_w + (h + 1) * DK_C]
            v = z_ref[0, rs, 2 * key_w + h * DV_C:2 * key_w + (h + 1) * DV_C]
            gate = z_ref[0, rs, 2 * key_w + H_C * DV_C + h * DV_C:2 * key_w + H_C * DV_C + (h + 1) * DV_C]
            lbh = lb[:, ks]
            g = jnp.log(lbh + (1.0 - lbh) * jax.nn.sigmoid(zf))
            k = (1.0 - lbh) * jax.nn.sigmoid(-zf)
            o, st_new = _gla_chunk(q, k, v, g, st_ref[h], tri, c=c, sub=sub)
            st_ref[h] = st_new
            o_ref[0, rs, h * DV_C:(h + 1) * DV_C] = (_rms(o, ng) * _silu(gate)).astype(o_ref.dtype)
        return carry

    lax.fori_loop(0, tt // c, chunk, 0)

    @pl.when(i == pl.num_programs(1) - 1)
    def _():
        for h in range(H_C):
            sout_ref[0, h] = st_ref[h].T


def _hgrn(z, lb_logits, s0, norm_gain, *, layer_i, tt, c, sub):
    bsz, s, _ = z.shape
    state = pl.BlockSpec((1, H_C, DK_C, DV_C), lambda b, i: (b, 0, 0, 0))
    width = 2 * H_C * DK_C + 2 * H_C * DV_C
    return pl.pallas_call(
        functools.partial(_hgrn_kernel, layer_i=layer_i, c=c, sub=sub, tt=tt),
        grid=(bsz, s // tt),
        in_specs=[pl.BlockSpec((1, tt, width), lambda b, i: (b, i, 0)),
                  pl.BlockSpec(lb_logits.shape, lambda b, i: (0, 0)),
                  state,
                  pl.BlockSpec((1, DV_C), lambda b, i: (0, 0))],
        out_specs=[pl.BlockSpec((1, tt, H_C * DV_C), lambda b, i: (b, i, 0)), state],
        out_shape=[jax.ShapeDtypeStruct((bsz, s, H_C * DV_C), BF16),
                   jax.ShapeDtypeStruct(s0.shape, F32)],
        scratch_shapes=[pltpu.VMEM((H_C, DV_C, DK_C), F32)],
        compiler_params=_cparams(2),
        name="hgrn2",
    )(z, lb_logits, s0, norm_gain)


def _gla_kernel(qk_ref, v_ref, r_ref, ad_ref, wg_ref, bg_ref, s0_ref, ng_ref, o_ref, sout_ref, st_ref,
                *, c, sub, tt):
    i = pl.program_id(1)

    @pl.when(i == 0)
    def _():
        for h in range(H_D):
            st_ref[h] = s0_ref[0, h].T

    tri = (lax.broadcasted_iota(jnp.int32, (c, c), 0) >= lax.broadcasted_iota(jnp.int32, (c, c), 1)).astype(F32)
    ng = ng_ref[...]
    key_w = H_D * DK_D
    scale = DK_D ** -0.5

    def chunk(ci, carry):
        r0 = pl.multiple_of(ci * c, c)
        rs = pl.ds(r0, c)
        za = _dot(ad_ref[0, rs, 0:LANES], wg_ref[...]) + bg_ref[...]
        lg = -_softplus(-za) * (1.0 / GLA_TAU)
        for h in range(H_D):
            ks = slice(h * DK_D, (h + 1) * DK_D)
            vs = slice(h * DV_D, (h + 1) * DV_D)
            q = qk_ref[0, rs, ks] * scale
            k = qk_ref[0, rs, key_w + h * DK_D:key_w + (h + 1) * DK_D]
            v = v_ref[0, rs, vs]
            o, st_new = _gla_chunk(q, k, v, lg[:, ks], st_ref[h], tri, c=c, sub=sub)
            st_ref[h] = st_new
            o_ref[0, rs, vs] = (_rms(o, ng) * _silu(r_ref[0, rs, vs])).astype(o_ref.dtype)
        return carry

    lax.fori_loop(0, tt // c, chunk, 0)

    @pl.when(i == pl.num_programs(1) - 1)
    def _():
        for h in range(H_D):
            sout_ref[0, h] = st_ref[h].T


def _gla(z, wg2p, bg, s0, norm_gain, *, tt, c, sub):
    bsz, s, _ = z.shape
    w = H_D * DV_D
    state = pl.BlockSpec((1, H_D, DK_D, DV_D), lambda b, i: (b, 0, 0, 0))

    def col(cb):
        return pl.BlockSpec((1, tt, w), lambda b, i: (b, i, cb))

    return pl.pallas_call(
        functools.partial(_gla_kernel, c=c, sub=sub, tt=tt),
        grid=(bsz, s // tt),
        in_specs=[col(4), col(5), col(6),
                  pl.BlockSpec((1, tt, CD_PAD - CD_MAIN), lambda b, i: (b, i, CD_MAIN // (CD_PAD - CD_MAIN))),
                  pl.BlockSpec(wg2p.shape, lambda b, i: (0, 0)),
                  pl.BlockSpec(bg.shape, lambda b, i: (0, 0)),
                  state,
                  pl.BlockSpec((1, DV_D), lambda b, i: (0, 0))],
        out_specs=[pl.BlockSpec((1, tt, w), lambda b, i: (b, i, 0)), state],
        out_shape=[jax.ShapeDtypeStruct((bsz, s, w), BF16),
                   jax.ShapeDtypeStruct(s0.shape, F32)],
        scratch_shapes=[pltpu.VMEM((H_D, DV_D, DK_D), F32)],
        compiler_params=_cparams(2),
        name="gla",
    )(z, z, z, z, wg2p, bg, s0, norm_gain)


def _out_proj_kernel(a1_ref, a2_ref, w1_ref, w2_ref, x_ref, o_ref):
    acc = _dot(a1_ref[...], w1_ref[...]) + _dot(a2_ref[...], w2_ref[...])
    o_ref[...] = x_ref[...] + acc


def _out_proj(a1, a2, w, x, *, layer, tm, tn):
    m, k1 = a1.shape
    k2 = a2.shape[1]
    assert k1 == k2 and w.shape[1] == k1 + k2
    n = w.shape[2]
    return pl.pallas_call(
        _out_proj_kernel,
        grid=(m // tm, n // tn),
        in_specs=[pl.BlockSpec((tm, k1), lambda i, j: (i, 0)),
                  pl.BlockSpec((tm, k2), lambda i, j: (i, 0)),
                  pl.BlockSpec((None, k1, tn), lambda i, j: (layer, 0, j)),
                  pl.BlockSpec((None, k2, tn), lambda i, j: (layer, 1, j)),
                  pl.BlockSpec((tm, tn), lambda i, j: (i, j))],
        out_specs=pl.BlockSpec((tm, tn), lambda i, j: (i, j)),
        out_shape=jax.ShapeDtypeStruct((m, n), F32),
        compiler_params=_cparams(2),
        name="out_proj",
    )(a1, a2, w, w, x)


def _ffn_kernel(x_ref, gain_ref, wg_ref, wu_ref, cw_ref, cb_ref, wd_ref, buf_ref, o_ref, nbuf_ref,
                xn_ref, acc_ref, gp_ref, carry_ref, *, nseg, seg, tiles_per_seq):
    i = pl.program_id(0)
    j = pl.program_id(1)
    halo = SUBLANES

    @pl.when(j == 0)
    def _():
        xn_ref[...] = _rms(x_ref[...], gain_ref[...]).astype(BF16)
        acc_ref[...] = jnp.zeros_like(acc_ref)

    xn = xn_ref[...]
    g = jnp.dot(xn, wg_ref[...], preferred_element_type=F32)
    u = jnp.dot(xn, wu_ref[...], preferred_element_type=F32)
    first = lax.rem(i, tiles_per_seq) == 0

    @pl.when(first)
    def _():
        for s in range(nseg):
            gp_ref[s, 0:halo, :] = buf_ref[s]

    @pl.when(jnp.logical_not(first))
    def _():
        for s in range(nseg):
            gp_ref[s, 0:halo, :] = carry_ref[j, s]

    for s in range(nseg):
        gs = g[s * seg:(s + 1) * seg]
        gp_ref[s, halo:halo + seg, :] = gs
        carry_ref[j, s] = gs[seg - halo:seg]
        nbuf_ref[s] = gs[seg - halo:seg]

    for s in range(nseg):
        gc = cb_ref[...]
        for t in range(CONV_F):
            off = halo - (CONV_F - 1) + t
            gc = gc + cw_ref[t:t + 1, :] * gp_ref[s, off:off + seg, :]
        act = _silu(gc) * u[s * seg:(s + 1) * seg]
        acc_ref[s * seg:(s + 1) * seg, :] += _dot(act, wd_ref[...])

    @pl.when(j == pl.num_programs(1) - 1)
    def _():
        o_ref[...] = x_ref[...] + acc_ref[...]


def _ffn(x, gain, p, buf8, *, layer, tm, tf, nseg, tiles_per_seq):
    m, d = x.shape
    seg = tm // nseg
    nj = D_FF_PAD // tf
    kern = functools.partial(_ffn_kernel, nseg=nseg, seg=seg, tiles_per_seq=tiles_per_seq)
    state = pl.BlockSpec((nseg, SUBLANES, tf), lambda i, j: (lax.div(i, tiles_per_seq), 0, j))
    tail = pl.BlockSpec((nseg, SUBLANES, tf), lambda i, j: (i, 0, j))
    n_tiles = m // tm
    tails = pl.pallas_call(
        kern,
        grid=(n_tiles, nj),
        in_specs=[pl.BlockSpec((tm, d), lambda i, j: (i, 0)),
                  pl.BlockSpec((1, d), lambda i, j: (0, 0)),
                  pl.BlockSpec((None, d, tf), lambda i, j: (layer, 0, j)),
                  pl.BlockSpec((None, d, tf), lambda i, j: (layer, 0, j)),
                  pl.BlockSpec((None, SUBLANES, tf), lambda i, j: (layer, 0, j)),
                  pl.BlockSpec((None, 1, tf), lambda i, j: (layer, 0, j)),
                  pl.BlockSpec((None, tf, d), lambda i, j: (layer, j, 0)),
                  state],
        out_specs=[pl.BlockSpec((tm, d), lambda i, j: (i, 0)), tail],
        out_shape=[jax.ShapeDtypeStruct((m, d), F32),
                   jax.ShapeDtypeStruct((n_tiles * nseg, SUBLANES, D_FF_PAD), F32)],
        scratch_shapes=[pltpu.VMEM((tm, d), BF16), pltpu.VMEM((tm, d), F32),
                        pltpu.VMEM((nseg, seg + SUBLANES, tf), F32),
                        pltpu.VMEM((nj, nseg, SUBLANES, tf), F32)],
        compiler_params=_cparams(2),
        name="conv_ffn",
    )(x, gain, p['wg'], p['wu'], p['cw'], p['cb'], p['wd'], buf8)
    y, nbuf = tails
    nbuf = nbuf.reshape(n_tiles // tiles_per_seq, tiles_per_seq, nseg, SUBLANES, D_FF_PAD)[:, -1]
    return y, nbuf.reshape(-1, SUBLANES, D_FF_PAD)


def _pad_to(a, axis, size):
    pad = [(0, 0)] * a.ndim
    pad[axis] = (0, size - a.shape[axis])
    return jnp.pad(a, pad)


def _tail_rows8(buf):
    return _pad_to(buf[:, ::-1], 1, SUBLANES)[:, ::-1]


def _stage_params(p):
    depth = p['norm_mix'].shape[0]
    up = p['ffn_w_up']
    ffn = {
        'wg': _pad_to(up[:, :, :D_FF], 2, D_FF_PAD).astype(BF16),
        'wu': _pad_to(up[:, :, D_FF:], 2, D_FF_PAD).astype(BF16),
        'cw': _pad_to(_pad_to(p['ffn_conv_w'], 2, D_FF_PAD), 1, SUBLANES),
        'cb': _pad_to(p['ffn_conv_b'][:, None, :], 2, D_FF_PAD),
        'wd': _pad_to(p['ffn_w_down'], 1, D_FF_PAD).astype(BF16),
    }
    w_in_ab = p['w_in_ab'].astype(BF16)
    w_out_ab = p['w_out_ab'].astype(BF16)
    w_in_cd = _pad_to(p['w_in_cd'], 2, CD_PAD).astype(BF16)
    w_out_cd = p['w_out_cd'].astype(BF16)
    layers = []
    for layer in range(depth):
        i = layer // 2
        lp = {
            'norm_mix': p['norm_mix'][layer][None],
            'norm_ffn': p['norm_ffn'][layer][None],
            'ffn': ffn,
        }
        if layer % 2 == 0:
            lp.update({
                'w_in': w_in_ab,
                'head_gain': jnp.stack([p['q_norm'][i], p['k_norm'][i]]),
                'w_out': w_out_ab,
                'lru': {
                    'cw': _pad_to(p['lru_conv_w'][i], 0, SUBLANES),
                    'cb': p['lru_conv_b'][i][None],
                    'wa': p['lru_wa'][i].astype(BF16),
                    'ba': p['lru_ba'][i][None],
                    'wx': p['lru_wx'][i].astype(BF16),
                    'bx': p['lru_bx'][i][None],
                    'lam': p['lru_lambda'][i][None],
                },
            })
        else:
            lp.update({
                'w_in': w_in_cd,
                'w_out': w_out_cd,
                'hgrn_norm': p['hgrn_norm'][i][None],
                'gla_norm': p['gla_norm'][i][None],
                'wg2': _pad_to(p['gla_wg2'][i], 0, LANES).astype(BF16),
                'bg': p['gla_bg'][i][None],
            })
        layers.append(lp)
    return layers


def _run_trunk(x, layers, lb_logits, st, cfg):
    bsz, s, d = x.shape
    m = bsz * s
    xf = x.reshape(m, d)
    no_heads = jnp.zeros((2, LANES), F32)
    kvs, convs, hs, shg, sgl, ffs = [], [], [], [], [], []
    for layer, lp in enumerate(layers):
        i = layer // 2
        if layer % 2 == 0:
            z = _norm_proj(xf, lp['norm_mix'], lp['w_in'], lp['head_gain'], layer=i, tm=cfg['tm'], tn=cfg['tn'],
                           n_head_tiles=W_A // cfg['tn'])
            z3 = z.reshape(bsz, s, -1)
            if st is None:
                o_a = _attn_prompt(z3).reshape(m, W_A)
                keep = min(DILATED_GROUPS[-1][0], s)
                kvs.append(z3[:, s - keep:, W_A:3 * W_A].reshape(bsz, keep, 2, H_A, DH_A))
                conv_buf = jnp.zeros((bsz, CONV_B - 1, W_B), F32)
                h0 = jnp.zeros((bsz, W_B), F32)
            else:
                cache = st[0]
                cache = cache.reshape(cache.shape[0], bsz, cache.shape[2], 2 * W_A)
                o_a = _attn_step(z3, cache, layer=i).reshape(m, W_A)
                kvs.append(z3[:, :, W_A:3 * W_A].reshape(bsz, s, 2, H_A, DH_A))
                conv_buf, h0 = st[1][i], st[2][i]
            o_b, nconv, hlast = _rglru(z3, _tail_rows8(conv_buf), h0[:, None, :], lp['lru'], tt=cfg['tt'])
            convs.append(nconv[:, SUBLANES - (CONV_B - 1):])
            hs.append(hlast[:, 0])
            xf = _out_proj(o_a, o_b.reshape(m, W_B), lp['w_out'], xf, layer=i, tm=cfg['tm'], tn=cfg['tn'])
        else:
            z = _norm_proj(xf, lp['norm_mix'], lp['w_in'], no_heads, layer=i, tm=cfg['tm'], tn=cfg['tn'],
                           n_head_tiles=0)
            z3 = z.reshape(bsz, s, -1)
            if st is None:
                s_c0 = jnp.zeros((bsz, H_C, DK_C, DV_C), F32)
                s_d0 = jnp.zeros((bsz, H_D, DK_D, DV_D), F32)
            else:
                s_c0, s_d0 = st[3][i], st[4][i]
            o_c, n_c = _hgrn(z3, lb_logits, s_c0, lp['hgrn_norm'], layer_i=i, tt=cfg['tg'],
                             c=cfg['chunk'], sub=cfg['sub'])
            o_d, n_d = _gla(z3, lp['wg2'], lp['bg'], s_d0, lp['gla_norm'], tt=cfg['tg'],
                            c=cfg['chunk'], sub=cfg['sub'])
            shg.append(n_c)
            sgl.append(n_d)
            xf = _out_proj(o_c.reshape(m, -1), o_d.reshape(m, -1), lp['w_out'], xf, layer=i, tm=cfg['tm'],
                           tn=cfg['tn'])
        fbuf = jnp.zeros((bsz, CONV_F - 1, D_FF), F32) if st is None else st[5][layer]
        buf8 = _pad_to(_tail_rows8(fbuf), 2, D_FF_PAD)
        xf, nbuf = _ffn(xf, lp['norm_ffn'], lp['ffn'], buf8, layer=layer, tm=cfg['ffn_tm'], tf=cfg['ffn_tf'],
                        nseg=cfg['ffn_nseg'], tiles_per_seq=cfg['ffn_tiles_per_seq'])
        ffs.append(nbuf[:, SUBLANES - (CONV_F - 1):, :D_FF])
    return (xf.reshape(bsz, s, d), jnp.stack(kvs), jnp.stack(convs), jnp.stack(hs), jnp.stack(shg),
            jnp.stack(sgl), jnp.stack(ffs))


def _config(bsz, s):
    m = bsz * s
    if s >= 512:
        return dict(tm=min(1024, m), tn=512, tt=512, tg=256, chunk=CHUNK, sub=SUB_CHUNK,
                    ffn_tm=512, ffn_tf=512, ffn_nseg=1, ffn_tiles_per_seq=s // 512)
    chunk = min(CHUNK, s)
    return dict(tm=m, tn=512, tt=s, tg=s, chunk=chunk, sub=min(SUB_CHUNK, chunk),
                ffn_tm=m, ffn_tf=512, ffn_nseg=bsz, ffn_tiles_per_seq=1)


def kernel(x_prompt, x_sample, cache_attn_kv, state_lru_conv, state_lru_h, state_hgrn, state_gla, state_ffn_conv, norm_mix, norm_ffn, w_in_ab, q_norm, k_norm, lru_conv_w, lru_conv_b, lru_wa, lru_ba, lru_wx, lru_bx, lru_lambda, w_out_ab, w_in_cd, hgrn_lb_logits, hgrn_norm, gla_wg2, gla_bg, gla_norm, w_out_cd, ffn_w_up, ffn_conv_w, ffn_conv_b, ffn_w_down):
    p = {
        'norm_mix': norm_mix, 'norm_ffn': norm_ffn, 'w_in_ab': w_in_ab, 'q_norm': q_norm, 'k_norm': k_norm,
        'lru_conv_w': lru_conv_w, 'lru_conv_b': lru_conv_b, 'lru_wa': lru_wa, 'lru_ba': lru_ba,
        'lru_wx': lru_wx, 'lru_bx': lru_bx, 'lru_lambda': lru_lambda, 'w_out_ab': w_out_ab,
        'w_in_cd': w_in_cd, 'hgrn_norm': hgrn_norm, 'gla_wg2': gla_wg2,
        'gla_bg': gla_bg, 'gla_norm': gla_norm, 'w_out_cd': w_out_cd, 'ffn_w_up': ffn_w_up,
        'ffn_conv_w': ffn_conv_w, 'ffn_conv_b': ffn_conv_b, 'ffn_w_down': ffn_w_down,
    }
    layers = _stage_params(p)
    y_p, p_kv, p_conv, p_h, p_hgrn, p_gla, p_ffn = _run_trunk(
        x_prompt, layers, hgrn_lb_logits, None, _config(*x_prompt.shape[:2]))
    st = (cache_attn_kv, state_lru_conv, state_lru_h, state_hgrn, state_gla, state_ffn_conv)
    y_s, s_kv, s_conv, s_h, s_hgrn, s_gla, s_ffn = _run_trunk(
        x_sample, layers, hgrn_lb_logits, st, _config(*x_sample.shape[:2]))
    return (y_p, y_s, p_kv, s_kv, p_conv, s_conv, p_h, s_h, p_hgrn, s_hgrn, p_gla, s_gla, p_ffn, s_ffn)
```

```python
import functools

import jax
import jax.numpy as jnp
from jax import lax
from jax.experimental import pallas as pl
from jax.experimental.pallas import tpu as pltpu

F32 = jnp.float32
BF16 = jnp.bfloat16
EPS = 1e-6
NEG_BIG = -1e30

LANES = 128
SUBLANES = 8
MXU_WIDTH = 256
VMEM_LIMIT = 56 * 1024 * 1024

H_A, DH_A = 8, 128
W_A = H_A * DH_A
DILATED_GROUPS = ((128, 1), (512, 4), (2048, 16))
ATTN_BLOCK = 128
W_B, H_B = 1024, 8
BH_B = W_B // H_B
CONV_B = 4
LRU_C = 8.0
H_C, DK_C, DV_C = 8, 128, 128
H_D, DK_D, DV_D = 4, 128, 256
GATE_RANK = 16
GLA_TAU = 16.0
CHUNK = 64
D_FF = 5504
D_FF_PAD = 5632
CONV_F = 3
CD_MAIN = 7168
CD_PAD = 7680
PROJ_TN = 512
FFN_TF = 512


def _cparams(n_axes):
    return pltpu.CompilerParams(dimension_semantics=("arbitrary",) * n_axes,
                                vmem_limit_bytes=VMEM_LIMIT)


def _dot(a, b):
    return jnp.dot(a.astype(BF16), b.astype(BF16), preferred_element_type=F32)


def _dot_nt(a, b):
    return lax.dot_general(a.astype(BF16), b.astype(BF16), (((1,), (1,)), ((), ())),
                           preferred_element_type=F32)


def _dot_tn(a, b):
    return lax.dot_general(a.astype(BF16), b.astype(BF16), (((0,), (0,)), ((), ())),
                           preferred_element_type=F32)


def _rms(x, gain):
    ms = jnp.mean(x * x, axis=-1, keepdims=True)
    return (x * lax.rsqrt(ms + EPS)) * gain


def _softplus(x):
    return jnp.maximum(x, 0.0) + jnp.log1p(jnp.exp(-jnp.abs(x)))


def _silu(x):
    return x * jax.nn.sigmoid(x)


def _gelu_tanh(x):
    return 0.5 * x * (1.0 + jnp.tanh(0.7978845608028654 * (x + 0.044715 * (x * x * x))))


def _norm_proj_kernel(x_ref, g_ref, w_ref, hg_ref, o_ref, xn_ref, *, n_head_tiles, tn):
    j = pl.program_id(1)

    @pl.when(j == 0)
    def _():
        xn_ref[...] = _rms(x_ref[...], g_ref[...]).astype(BF16)

    acc = jnp.dot(xn_ref[...], w_ref[...], preferred_element_type=F32)
    if n_head_tiles == 0:
        o_ref[...] = acc
        return

    @pl.when(j >= 2 * n_head_tiles)
    def _():
        o_ref[...] = acc

    @pl.when(j < 2 * n_head_tiles)
    def _():
        gain = hg_ref[pl.ds(lax.div(j, n_head_tiles), 1), :]
        for h in range(tn // LANES):
            sl = slice(h * LANES, (h + 1) * LANES)
            o_ref[:, sl] = _rms(acc[:, sl], gain)


def _norm_proj(x, gain, w, head_gain, *, layer, tm, n_head_tiles):
    m, d = x.shape
    tn = PROJ_TN
    n = w.shape[2]
    kern = functools.partial(_norm_proj_kernel, n_head_tiles=n_head_tiles, tn=tn)
    return pl.pallas_call(
        kern,
        grid=(m // tm, n // tn),
        in_specs=[
            pl.BlockSpec((tm, d), lambda i, j: (i, 0)),
            pl.BlockSpec((1, d), lambda i, j: (0, 0)),
            pl.BlockSpec((None, d, tn), lambda i, j: (layer, 0, j)),
            pl.BlockSpec(head_gain.shape, lambda i, j: (0, 0)),
        ],
        out_specs=pl.BlockSpec((tm, tn), lambda i, j: (i, j)),
        out_shape=jax.ShapeDtypeStruct((m, n), F32),
        scratch_shapes=[pltpu.VMEM((tm, d), BF16)],
        compiler_params=_cparams(2),
        name="norm_proj",
    )(x, gain, w, head_gain)


ATTN_MAX_STRIDE = 4


def _ds(start, size, stride):
    return pl.ds(start, size, stride=stride) if stride > 1 else pl.ds(start, size)


def _attn_group(load, store, d, span, n_keys, first_span):
    qb = ATTN_BLOCK
    row = lax.broadcasted_iota(jnp.int32, (qb, 2 * qb), 0)
    col = lax.broadcasted_iota(jnp.int32, (qb, 2 * qb), 1)
    scale = DH_A ** -0.5
    band = (col >= row + (qb - n_keys)) & (col <= row + qb)
    band_first = band & (jnp.logical_not(first_span) | (col >= qb))
    blk = qb * d
    for sb in range(span // blk):
        valid = band_first if sb == 0 else band
        for r in range(d):
            q0 = sb * blk + r
            q = load('q', q0, qb, d)
            if sb == 0:
                k = jnp.concatenate([load('kp', span - blk + r, qb, d), load('kc', r, qb, d)], axis=0)
                v = jnp.concatenate([load('vp', span - blk + r, qb, d), load('vc', r, qb, d)], axis=0)
            else:
                k = load('kc', q0 - blk, 2 * qb, d)
                v = load('vc', q0 - blk, 2 * qb, d)
            s = jnp.where(valid, _dot_nt(q, k) * scale, NEG_BIG)
            m = jnp.max(s, axis=-1, keepdims=True)
            p = jnp.where(valid, jnp.exp(s - m), 0.0)
            den = jnp.sum(p, axis=-1, keepdims=True)
            store(q0, qb, d, _dot(p, v) / den, jnp.broadcast_to(m + jnp.log(den), (qb, DH_A)))


def _attn_span_kernel(q_ref, kp_ref, kc_ref, vp_ref, vc_ref, o_ref, oacc_ref, lacc_ref, stage_ref, so_ref,
                      *, span):
    first_span = pl.program_id(1) == 0
    names = ('q', 'kp', 'kc', 'vp', 'vc')
    srcs = dict(zip(names, (q_ref, kp_ref, kc_ref, vp_ref, vc_ref)))
    for g, (window, d) in enumerate(DILATED_GROUPS):
        n_keys = window // d
        if d <= ATTN_MAX_STRIDE:
            def load(name, start, size, stride):
                return srcs[name][0, _ds(start, size, stride), :]

            def store(start, size, stride, o, lse, g=g):
                oacc_ref[g, _ds(start, size, stride), :] = o
                lacc_ref[g, _ds(start, size, stride), :] = lse

            _attn_group(load, store, d, span, n_keys, first_span)
            continue
        d1 = ATTN_MAX_STRIDE
        sub = span // d1
        for r1 in range(d1):
            for a, name in enumerate(names):
                stage_ref[a] = srcs[name][0, _ds(r1, sub, d1), :]

            def load(name, start, size, stride):
                return stage_ref[names.index(name), _ds(start, size, stride), :]

            def store(start, size, stride, o, lse):
                so_ref[0, _ds(start, size, stride), :] = o
                so_ref[1, _ds(start, size, stride), :] = lse

            _attn_group(load, store, d // d1, sub, n_keys, first_span)
            oacc_ref[g, _ds(r1, sub, d1), :] = so_ref[0]
            lacc_ref[g, _ds(r1, sub, d1), :] = so_ref[1]
    la, lb, lc = lacc_ref[0], lacc_ref[1], lacc_ref[2]
    m = jnp.maximum(jnp.maximum(la, lb), lc)
    ea, eb, ec = jnp.exp(la - m), jnp.exp(lb - m), jnp.exp(lc - m)
    den = ea + eb + ec
    o_ref[0] = ((ea / den) * oacc_ref[0] + (eb / den) * oacc_ref[1] + (ec / den) * oacc_ref[2]).astype(o_ref.dtype)


def _attn_prompt(z):
    bsz, s, _ = z.shape
    span = max(d for _, d in DILATED_GROUPS) * ATTN_BLOCK
    assert s % span == 0 and len(DILATED_GROUPS) == 3
    assert all(w // d <= ATTN_BLOCK and span % (d * ATTN_BLOCK) == 0 for w, d in DILATED_GROUPS)
    blk = (1, span, DH_A)

    def cur(c):
        return pl.BlockSpec(blk, lambda b, n, h: (b, n, c * H_A + h))

    def prev(c):
        return pl.BlockSpec(blk, lambda b, n, h: (b, jnp.maximum(n - 1, 0), c * H_A + h))

    return pl.pallas_call(
        functools.partial(_attn_span_kernel, span=span),
        grid=(bsz, s // span, H_A),
        in_specs=[cur(0), prev(1), cur(1), prev(2), cur(2)],
        out_specs=pl.BlockSpec(blk, lambda b, n, h: (b, n, h)),
        out_shape=jax.ShapeDtypeStruct((bsz, s, W_A), BF16),
        scratch_shapes=[pltpu.VMEM((3, span, DH_A), F32), pltpu.VMEM((3, span, DH_A), F32),
                        pltpu.VMEM((5, span // ATTN_MAX_STRIDE, DH_A), F32),
                        pltpu.VMEM((2, span // ATTN_MAX_STRIDE, DH_A), F32)],
        compiler_params=_cparams(3),
        name="attn_prompt",
    )(z, z, z, z, z)


def _attn_step_kernel(z_ref, kc_ref, vc_ref, o_ref, *, cache_len, t_new):
    scale = DH_A ** -0.5
    nq = H_A * t_new

    def heads(c0):
        return jnp.concatenate([z_ref[0, :, c0 + h * DH_A:c0 + (h + 1) * DH_A] for h in range(H_A)], axis=0)

    q, kn, vn = heads(0), heads(W_A), heads(2 * W_A)
    sn = _dot_nt(q, kn) * scale
    log_t, log_h = t_new.bit_length() - 1, H_A.bit_length() - 1
    rn = lax.broadcasted_iota(jnp.int32, (nq, nq), 0)
    cn = lax.broadcasted_iota(jnp.int32, (nq, nq), 1)
    head_n = (rn >> log_t) == (cn >> log_t)
    dn = (rn & (t_new - 1)) - (cn & (t_new - 1))
    outs, lses = [], []
    for window, d in DILATED_GROUPS:
        assert d & (d - 1) == 0
        rows = min(window, cache_len)
        nc = rows * H_A
        kc = kc_ref[0, cache_len - rows:cache_len].reshape(nc, DH_A)
        vc = vc_ref[0, cache_len - rows:cache_len].reshape(nc, DH_A)
        sc = _dot_nt(q, kc) * scale
        rc = lax.broadcasted_iota(jnp.int32, (nq, nc), 0)
        cc = lax.broadcasted_iota(jnp.int32, (nq, nc), 1)
        dc = rows + (rc & (t_new - 1)) - (cc >> log_h)
        okc = ((rc >> log_t) == (cc & (H_A - 1))) & (dc <= window) & ((dc & (d - 1)) == 0)
        okn = head_n & (dn >= 0) & (dn <= window) & ((dn & (d - 1)) == 0)
        m = jnp.maximum(jnp.max(jnp.where(okc, sc, NEG_BIG), axis=-1, keepdims=True),
                        jnp.max(jnp.where(okn, sn, NEG_BIG), axis=-1, keepdims=True))
        pc = jnp.where(okc, jnp.exp(sc - m), 0.0)
        pn = jnp.where(okn, jnp.exp(sn - m), 0.0)
        den = jnp.sum(pc, axis=-1, keepdims=True) + jnp.sum(pn, axis=-1, keepdims=True)
        outs.append((_dot(pc, vc) + _dot(pn, vn)) / den)
        lses.append(m + jnp.log(den))
    m = jnp.maximum(jnp.maximum(lses[0], lses[1]), lses[2])
    es = [jnp.exp(l - m) for l in lses]
    den = es[0] + es[1] + es[2]
    o = (es[0] / den) * outs[0] + (es[1] / den) * outs[1] + (es[2] / den) * outs[2]
    for h in range(H_A):
        o_ref[0, :, h * DH_A:(h + 1) * DH_A] = o[h * t_new:(h + 1) * t_new]


def _attn_step(z, cache, *, layer):
    bsz, t_new, zc = z.shape
    cache_len = cache.shape[2]
    assert t_new == SUBLANES and H_A & (H_A - 1) == 0
    old = (None, 1, cache_len, None, H_A, DH_A)
    return pl.pallas_call(
        functools.partial(_attn_step_kernel, cache_len=cache_len, t_new=t_new),
        grid=(bsz,),
        in_specs=[
            pl.BlockSpec((1, t_new, zc), lambda b: (b, 0, 0)),
            pl.BlockSpec(old, lambda b: (layer, b, 0, 0, 0, 0)),
            pl.BlockSpec(old, lambda b: (layer, b, 0, 1, 0, 0)),
        ],
        out_specs=pl.BlockSpec((1, t_new, W_A), lambda b: (b, 0, 0)),
        out_shape=jax.ShapeDtypeStruct((bsz, t_new, W_A), F32),
        compiler_params=_cparams(1),
        name="attn_step",
    )(z, cache, cache)


def _rglru_kernel(xb_ref, gb_ref, cbuf_ref, h0_ref, cw_ref, cb_ref, wa_ref, ba_ref, wx_ref, bx_ref,
                  lam_ref, ob_ref, nconv_ref, hlast_ref, xp_ref, a_ref, u_ref, hs_ref, hc_ref, *, tt):
    i = pl.program_id(1)
    halo = SUBLANES

    @pl.when(i == 0)
    def _():
        xp_ref[0:halo, :] = cbuf_ref[0]
        hc_ref[...] = h0_ref[0]

    @pl.when(i > 0)
    def _():
        xp_ref[0:halo, :] = xp_ref[tt:tt + halo, :]

    xp_ref[halo:halo + tt, :] = xb_ref[0]
    nconv_ref[0] = xp_ref[tt:tt + halo, :]

    xc = cb_ref[...]
    for j in range(CONV_B):
        off = halo - (CONV_B - 1) + j
        xc = xc + cw_ref[j:j + 1, :] * xp_ref[off:off + tt, :]

    for h in range(H_B):
        sl = slice(h * BH_B, (h + 1) * BH_B)
        xh = xc[:, sl]
        r = jax.nn.sigmoid(_dot(xh, wa_ref[h]) + ba_ref[:, sl])
        ig = jax.nn.sigmoid(_dot(xh, wx_ref[h]) + bx_ref[:, sl])
        log_a = -LRU_C * r * _softplus(-lam_ref[:, sl])
        a_ref[:, sl] = jnp.exp(log_a)
        one_minus_a2 = -jnp.tanh(log_a) * (jnp.exp(2.0 * log_a) + 1.0)
        u_ref[:, sl] = jnp.sqrt(one_minus_a2) * (ig * xh)

    def block(kb, h):
        base = pl.multiple_of(kb * SUBLANES, SUBLANES)
        a8 = a_ref[pl.ds(base, SUBLANES), :]
        u8 = u_ref[pl.ds(base, SUBLANES), :]
        rows = []
        for r in range(SUBLANES):
            h = a8[r:r + 1] * h + u8[r:r + 1]
            rows.append(h)
        hs_ref[pl.ds(base, SUBLANES), :] = jnp.concatenate(rows, axis=0)
        return h

    h = lax.fori_loop(0, tt // SUBLANES, block, hc_ref[...])
    hc_ref[...] = h
    hlast_ref[0] = h
    ob_ref[0] = (hs_ref[...] * _gelu_tanh(gb_ref[0])).astype(ob_ref.dtype)


def _rglru(z, conv_buf8, h0, p, *, tt):
    bsz, s, _ = z.shape
    tile = pl.BlockSpec((1, tt, W_B), lambda b, i: (b, i, 3))
    gate = pl.BlockSpec((1, tt, W_B), lambda b, i: (b, i, 4))
    per_b8 = pl.BlockSpec((1, SUBLANES, W_B), lambda b, i: (b, 0, 0))
    per_b1 = pl.BlockSpec((1, 1, W_B), lambda b, i: (b, 0, 0))

    def whole(a):
        return pl.BlockSpec(a.shape, lambda b, i: (0,) * a.ndim)

    weights = [p['cw'], p['cb'], p['wa'], p['ba'], p['wx'], p['bx'], p['lam']]
    return pl.pallas_call(
        functools.partial(_rglru_kernel, tt=tt),
        grid=(bsz, s // tt),
        in_specs=[tile, gate, per_b8, per_b1] + [whole(a) for a in weights],
        out_specs=[pl.BlockSpec((1, tt, W_B), lambda b, i: (b, i, 0)), per_b8, per_b1],
        out_shape=[jax.ShapeDtypeStruct((bsz, s, W_B), BF16),
                   jax.ShapeDtypeStruct((bsz, SUBLANES, W_B), F32),
                   jax.ShapeDtypeStruct((bsz, 1, W_B), F32)],
        scratch_shapes=[pltpu.VMEM((tt + SUBLANES, W_B), F32), pltpu.VMEM((tt, W_B), F32),
                        pltpu.VMEM((tt, W_B), F32), pltpu.VMEM((tt, W_B), F32),
                        pltpu.VMEM((1, W_B), F32)],
        compiler_params=_cparams(2),
        name="rglru",
    )(z, z, conv_buf8, h0, *weights)


GLA_PACK = 4
GLA_BATCH = 2


def _pair_masks(c):
    wide = GLA_PACK * c
    t = lax.broadcasted_iota(jnp.int32, (c, wide), 0)
    col = lax.broadcasted_iota(jnp.int32, (c, wide), 1)
    s = col & (c - 1)
    pairs = [t == s]
    m = c // 2
    while m >= 1:
        same_block = (t & -(2 * m)) == (s & -(2 * m))
        pairs.append(same_block & ((t & m) != 0) & ((s & m) == 0))
        m //= 2
    log_c = c.bit_length() - 1
    return [p & ((col >> log_c) == (j % GLA_PACK)) for j, p in enumerate(pairs)]


def _cumsum_rows(tri, g):
    hi = g.astype(BF16)
    r1 = g - hi.astype(F32)
    mid = r1.astype(BF16)
    lo = (r1 - mid.astype(F32)).astype(BF16)
    return (jnp.dot(tri, hi, preferred_element_type=F32) + jnp.dot(tri, mid, preferred_element_type=F32)
            + jnp.dot(tri, lo, preferred_element_type=F32))


def _level_ref(b, m):
    c, kdim = b.shape
    size = 2 * m
    if m >= SUBLANES:
        parts = [jnp.broadcast_to(b[j * size + m - 1:j * size + m], (size, kdim)) for j in range(c // size)]
        return parts[0] if len(parts) == 1 else jnp.concatenate(parts, axis=0)
    b3 = b.reshape(c // SUBLANES, SUBLANES, kdim)
    sub = lax.broadcasted_iota(jnp.int32, b3.shape, 1)
    ref = None
    for r0 in range(m - 1, SUBLANES, size):
        cand = jnp.broadcast_to(b3[:, r0:r0 + 1, :], b3.shape)
        ref = cand if ref is None else jnp.where(sub >= r0 - (m - 1), cand, ref)
    return ref.reshape(c, kdim)


def _gla_chunk(q, k, v, b, st, masks):
    c = q.shape[0]
    o = _dot_nt(q * jnp.exp(b), st)
    scaled = [(q, k)]
    m = c // 2
    while m >= 1:
        e = jnp.exp(-jnp.abs(b - _level_ref(b, m)))
        scaled.append((q * e, k * e))
        m //= 2
    acc = None
    for j0 in range(0, len(scaled), GLA_PACK):
        grp = scaled[j0:j0 + GLA_PACK]
        grp = grp + [scaled[0]] * (GLA_PACK - len(grp))
        p = _dot_nt(jnp.concatenate([qs for qs, _ in grp], axis=0),
                    jnp.concatenate([ks for _, ks in grp], axis=0))
        for j in range(j0, min(j0 + GLA_PACK, len(scaled))):
            blk = jnp.where(masks[j], p[(j - j0) * c:(j - j0 + 1) * c], 0.0)
            acc = blk if acc is None else acc + blk
    o = o + _dot(acc, jnp.concatenate([v] * GLA_PACK, axis=0))
    blast = b[c - 1:c]
    kd = k * jnp.exp(blast - b)
    st_new = st * jnp.exp(blast) + _dot_tn(v, kd)
    return o, st_new


def _hgrn_kernel(z_ref, lbl_ref, s0_ref, ng_ref, o_ref, sout_ref, st_ref, *, layer_i, c, tt):
    i = pl.program_id(1)
    nb = z_ref.shape[0]

    @pl.when(i == 0)
    def _():
        for bi in range(nb):
            for h in range(H_C):
                st_ref[bi * H_C + h] = s0_ref[bi, h].T

    lg = lbl_ref[...]
    e = jnp.exp(lg - jnp.max(lg, axis=0, keepdims=True))
    pr = e / jnp.sum(e, axis=0, keepdims=True)
    acc = pr[0:1]
    for l in range(1, layer_i + 1):
        acc = acc + pr[l:l + 1]
    lb = acc - pr[0:1]
    tri = (lax.broadcasted_iota(jnp.int32, (c, c), 0) >= lax.broadcasted_iota(jnp.int32, (c, c), 1)).astype(BF16)
    masks = _pair_masks(c)
    ng = ng_ref[...]
    key_w = H_C * DK_C

    def chunk(ci, carry):
        r0 = pl.multiple_of(ci * c, c)
        rs = pl.ds(r0, c)
        for bi in range(nb):
            zf_all = z_ref[bi, rs, key_w:2 * key_w]
            b_all = _cumsum_rows(tri, jnp.log(lb + (1.0 - lb) * jax.nn.sigmoid(zf_all)))
            for h in range(H_C):
                ks = slice(h * DK_C, (h + 1) * DK_C)
                q = z_ref[bi, rs, ks]
                zf = z_ref[bi, rs, key_w + h * DK_C:key_w + (h + 1) * DK_C]
                v = z_ref[bi, rs, 2 * key_w + h * DV_C:2 * key_w + (h + 1) * DV_C]
                gate = z_ref[bi, rs, 2 * key_w + H_C * DV_C + h * DV_C:2 * key_w + H_C * DV_C + (h + 1) * DV_C]
                k = (1.0 - lb[:, ks]) * jax.nn.sigmoid(-zf)
                o, st_new = _gla_chunk(q, k, v, b_all[:, ks], st_ref[bi * H_C + h], masks)
                st_ref[bi * H_C + h] = st_new
                o_ref[bi, rs, h * DV_C:(h + 1) * DV_C] = (_rms(o, ng) * _silu(gate)).astype(o_ref.dtype)
        return carry

    lax.fori_loop(0, tt // c, chunk, 0)

    @pl.when(i == pl.num_programs(1) - 1)
    def _():
        for bi in range(nb):
            for h in range(H_C):
                sout_ref[bi, h] = st_ref[bi * H_C + h].T


def _hgrn(z, lb_logits, s0, norm_gain, *, layer_i, tt, c):
    bsz, s, _ = z.shape
    nb = GLA_BATCH
    state = pl.BlockSpec((nb, H_C, DK_C, DV_C), lambda b, i: (b, 0, 0, 0))
    width = 2 * H_C * DK_C + 2 * H_C * DV_C
    return pl.pallas_call(
        functools.partial(_hgrn_kernel, layer_i=layer_i, c=c, tt=tt),
        grid=(bsz // nb, s // tt),
        in_specs=[pl.BlockSpec((nb, tt, width), lambda b, i: (b, i, 0)),
                  pl.BlockSpec(lb_logits.shape, lambda b, i: (0, 0)),
                  state,
                  pl.BlockSpec((1, DV_C), lambda b, i: (0, 0))],
        out_specs=[pl.BlockSpec((nb, tt, H_C * DV_C), lambda b, i: (b, i, 0)), state],
        out_shape=[jax.ShapeDtypeStruct((bsz, s, H_C * DV_C), BF16),
                   jax.ShapeDtypeStruct(s0.shape, F32)],
        scratch_shapes=[pltpu.VMEM((nb * H_C, DV_C, DK_C), F32)],
        compiler_params=_cparams(2),
        name="hgrn2",
    )(z, lb_logits, s0, norm_gain)


def _gla_kernel(qk_ref, v_ref, r_ref, ad_ref, wg_ref, bg_ref, s0_ref, ng_ref, o_ref, sout_ref, st_ref,
                *, c, tt):
    i = pl.program_id(1)
    nb = qk_ref.shape[0]

    @pl.when(i == 0)
    def _():
        for bi in range(nb):
            for h in range(H_D):
                st_ref[bi * H_D + h] = s0_ref[bi, h].T

    tri = (lax.broadcasted_iota(jnp.int32, (c, c), 0) >= lax.broadcasted_iota(jnp.int32, (c, c), 1)).astype(BF16)
    masks = _pair_masks(c)
    ng = ng_ref[...]
    key_w = H_D * DK_D
    scale = DK_D ** -0.5

    def chunk(ci, carry):
        r0 = pl.multiple_of(ci * c, c)
        rs = pl.ds(r0, c)
        for bi in range(nb):
            za = _dot(ad_ref[bi, rs, 0:LANES], wg_ref[...]) + bg_ref[...]
            b_all = _cumsum_rows(tri, -_softplus(-za) * (1.0 / GLA_TAU))
            for h in range(H_D):
                ks = slice(h * DK_D, (h + 1) * DK_D)
                vs = slice(h * DV_D, (h + 1) * DV_D)
                q = qk_ref[bi, rs, ks] * scale
                k = qk_ref[bi, rs, key_w + h * DK_D:key_w + (h + 1) * DK_D]
                v = v_ref[bi, rs, vs]
                o, st_new = _gla_chunk(q, k, v, b_all[:, ks], st_ref[bi * H_D + h], masks)
                st_ref[bi * H_D + h] = st_new
                o_ref[bi, rs, vs] = (_rms(o, ng) * _silu(r_ref[bi, rs, vs])).astype(o_ref.dtype)
        return carry

    lax.fori_loop(0, tt // c, chunk, 0)

    @pl.when(i == pl.num_programs(1) - 1)
    def _():
        for bi in range(nb):
            for h in range(H_D):
                sout_ref[bi, h] = st_ref[bi * H_D + h].T


def _gla(z, wg2p, bg, s0, norm_gain, *, tt, c):
    bsz, s, _ = z.shape
    w = H_D * DV_D
    nb = GLA_BATCH
    state = pl.BlockSpec((nb, H_D, DK_D, DV_D), lambda b, i: (b, 0, 0, 0))

    def col(cb):
        return pl.BlockSpec((nb, tt, w), lambda b, i: (b, i, cb))

    return pl.pallas_call(
        functools.partial(_gla_kernel, c=c, tt=tt),
        grid=(bsz // nb, s // tt),
        in_specs=[col(4), col(5), col(6),
                  pl.BlockSpec((nb, tt, CD_PAD - CD_MAIN), lambda b, i: (b, i, CD_MAIN // (CD_PAD - CD_MAIN))),
                  pl.BlockSpec(wg2p.shape, lambda b, i: (0, 0)),
                  pl.BlockSpec(bg.shape, lambda b, i: (0, 0)),
                  state,
                  pl.BlockSpec((1, DV_D), lambda b, i: (0, 0))],
        out_specs=[pl.BlockSpec((nb, tt, w), lambda b, i: (b, i, 0)), state],
        out_shape=[jax.ShapeDtypeStruct((bsz, s, w), BF16),
                   jax.ShapeDtypeStruct(s0.shape, F32)],
        scratch_shapes=[pltpu.VMEM((nb * H_D, DV_D, DK_D), F32)],
        compiler_params=_cparams(2),
        name="gla",
    )(z, z, z, z, wg2p, bg, s0, norm_gain)


def _out_proj_kernel(a1_ref, a2_ref, w1_ref, w2_ref, x_ref, o_ref):
    acc = _dot(a1_ref[...], w1_ref[...]) + _dot(a2_ref[...], w2_ref[...])
    o_ref[...] = x_ref[...] + acc


def _out_proj(a1, a2, w, x, *, layer, tm, tn):
    m, k1 = a1.shape
    k2 = a2.shape[1]
    assert k1 == k2 and w.shape[1] == k1 + k2
    n = w.shape[2]
    return pl.pallas_call(
        _out_proj_kernel,
        grid=(m // tm, n // tn),
        in_specs=[pl.BlockSpec((tm, k1), lambda i, j: (i, 0)),
                  pl.BlockSpec((tm, k2), lambda i, j: (i, 0)),
                  pl.BlockSpec((None, k1, tn), lambda i, j: (layer, 0, j)),
                  pl.BlockSpec((None, k2, tn), lambda i, j: (layer, 1, j)),
                  pl.BlockSpec((tm, tn), lambda i, j: (i, j))],
        out_specs=pl.BlockSpec((tm, tn), lambda i, j: (i, j)),
        out_shape=jax.ShapeDtypeStruct((m, n), F32),
        compiler_params=_cparams(2),
        name="out_proj",
    )(a1, a2, w, w, x)


def _ffn_kernel(x_ref, gain_ref, wg_ref, wu_ref, cw_ref, cb_ref, wd_ref, buf_ref, o_ref, nbuf_ref,
                xn_ref, acc_ref, gp_ref, carry_ref, *, nseg, seg, tiles_per_seq, chain):
    i = pl.program_id(0)
    j = pl.program_id(1)
    halo = SUBLANES

    @pl.when(j == 0)
    def _():
        xn_ref[...] = _rms(x_ref[...], gain_ref[...]).astype(BF16)
        acc_ref[...] = jnp.zeros_like(acc_ref)

    xn = xn_ref[...]
    first = lax.rem(i, tiles_per_seq) == 0
    down = None
    for c in range(wg_ref.shape[1] // chain):
        cs = slice(c * chain, (c + 1) * chain)
        g = jnp.dot(xn, wg_ref[:, cs], preferred_element_type=F32)
        u = jnp.dot(xn, wu_ref[:, cs], preferred_element_type=F32)
        acts = []
        for s in range(nseg):
            gs = g[s * seg:(s + 1) * seg]
            gp_ref[s, 0:halo, cs] = jnp.where(first, buf_ref[s, :, cs], carry_ref[j, s, :, cs])
            gp_ref[s, halo:halo + seg, cs] = gs
            carry_ref[j, s, :, cs] = gs[seg - halo:seg]
            nbuf_ref[s, :, cs] = gs[seg - halo:seg]
            gc = cb_ref[:, cs]
            for t in range(CONV_F):
                off = halo - (CONV_F - 1) + t
                gc = gc + cw_ref[t:t + 1, cs] * gp_ref[s, off:off + seg, cs]
            acts.append(_silu(gc) * u[s * seg:(s + 1) * seg])
        act = acts[0] if nseg == 1 else jnp.concatenate(acts, axis=0)
        d = _dot(act, wd_ref[cs, :])
        down = d if down is None else down + d
    acc_ref[...] += down

    @pl.when(j == pl.num_programs(1) - 1)
    def _():
        o_ref[...] = x_ref[...] + acc_ref[...]


def _ffn(x, gain, p, buf8, *, layer, tm, tf, nseg, tiles_per_seq):
    m, d = x.shape
    seg = tm // nseg
    nj = D_FF_PAD // tf
    kern = functools.partial(_ffn_kernel, nseg=nseg, seg=seg, tiles_per_seq=tiles_per_seq, chain=MXU_WIDTH)
    state = pl.BlockSpec((nseg, SUBLANES, tf), lambda i, j: (lax.div(i, tiles_per_seq), 0, j))
    tail = pl.BlockSpec((nseg, SUBLANES, tf), lambda i, j: (i, 0, j))
    n_tiles = m // tm
    tails = pl.pallas_call(
        kern,
        grid=(n_tiles, nj),
        in_specs=[pl.BlockSpec((tm, d), lambda i, j: (i, 0)),
                  pl.BlockSpec((1, d), lambda i, j: (0, 0)),
                  pl.BlockSpec((None, d, tf), lambda i, j: (layer, 0, j)),
                  pl.BlockSpec((None, d, tf), lambda i, j: (layer, 0, j)),
                  pl.BlockSpec((None, SUBLANES, tf), lambda i, j: (layer, 0, j)),
                  pl.BlockSpec((None, 1, tf), lambda i, j: (layer, 0, j)),
                  pl.BlockSpec((None, tf, d), lambda i, j: (layer, j, 0)),
                  state],
        out_specs=[pl.BlockSpec((tm, d), lambda i, j: (i, 0)), tail],
        out_shape=[jax.ShapeDtypeStruct((m, d), F32),
                   jax.ShapeDtypeStruct((n_tiles * nseg, SUBLANES, D_FF_PAD), F32)],
        scratch_shapes=[pltpu.VMEM((tm, d), BF16), pltpu.VMEM((tm, d), F32),
                        pltpu.VMEM((nseg, seg + SUBLANES, tf), F32),
                        pltpu.VMEM((nj, nseg, SUBLANES, tf), F32)],
        compiler_params=_cparams(2),
        name="conv_ffn",
    )(x, gain, p['wg'], p['wu'], p['cw'], p['cb'], p['wd'], buf8)
    y, nbuf = tails
    nbuf = nbuf.reshape(n_tiles // tiles_per_seq, tiles_per_seq, nseg, SUBLANES, D_FF_PAD)[:, -1]
    return y, nbuf.reshape(-1, SUBLANES, D_FF_PAD)


def _pad_to(a, axis, size):
    pad = [(0, 0)] * a.ndim
    pad[axis] = (0, size - a.shape[axis])
    return jnp.pad(a, pad)


def _tail_rows8(buf):
    return jnp.pad(buf, ((0, 0), (SUBLANES - buf.shape[1], 0), (0, 0)))


def _cast_pad_cols_kernel(*refs, n_in, ncols):
    for src, dst in zip(refs[:n_in], refs[n_in:]):
        dst[:, :ncols] = src[...].astype(BF16)
        dst[:, ncols:] = jnp.zeros((dst.shape[0], dst.shape[1] - ncols), BF16)


def _cast_pad_cols(w, col_blocks, ncols, padded, *, tr):
    layers, rows, _ = w.shape
    n_in = len(col_blocks)
    assert ncols % LANES == 0 and padded % LANES == 0
    return pl.pallas_call(
        functools.partial(_cast_pad_cols_kernel, n_in=n_in, ncols=ncols),
        grid=(layers, rows // tr),
        in_specs=[pl.BlockSpec((None, tr, ncols), functools.partial(lambda cb, l, i: (l, i, cb), cb))
                  for cb in col_blocks],
        out_specs=[pl.BlockSpec((None, tr, padded), lambda l, i: (l, i, 0))] * n_in,
        out_shape=[jax.ShapeDtypeStruct((layers, rows, padded), BF16)] * n_in,
        compiler_params=_cparams(2),
        name="stage_cols",
    )(*([w] * n_in))


def _stage_params(p):
    depth = p['norm_mix'].shape[0]
    wg, wu = _cast_pad_cols(p['ffn_w_up'], (0, 1), D_FF, D_FF_PAD, tr=256)
    ffn = {
        'wg': wg,
        'wu': wu,
        'cw': _pad_to(_pad_to(p['ffn_conv_w'], 2, D_FF_PAD), 1, SUBLANES),
        'cb': _pad_to(p['ffn_conv_b'][:, None, :], 2, D_FF_PAD),
        'wd': _pad_to(p['ffn_w_down'].astype(BF16), 1, D_FF_PAD),
    }
    w_in_ab = p['w_in_ab'].astype(BF16)
    w_out_ab = p['w_out_ab'].astype(BF16)
    w_in_cd = _pad_to(p['w_in_cd'].astype(BF16), 2, CD_PAD)
    w_out_cd = p['w_out_cd'].astype(BF16)
    layers = []
    for layer in range(depth):
        i = layer // 2
        lp = {
            'norm_mix': p['norm_mix'][layer][None],
            'norm_ffn': p['norm_ffn'][layer][None],
            'ffn': ffn,
        }
        if layer % 2 == 0:
            lp.update({
                'w_in': w_in_ab,
                'head_gain': jnp.stack([p['q_norm'][i], p['k_norm'][i]]),
                'w_out': w_out_ab,
                'lru': {
                    'cw': _pad_to(p['lru_conv_w'][i], 0, SUBLANES),
                    'cb': p['lru_conv_b'][i][None],
                    'wa': p['lru_wa'][i].astype(BF16),
                    'ba': p['lru_ba'][i][None],
                    'wx': p['lru_wx'][i].astype(BF16),
                    'bx': p['lru_bx'][i][None],
                    'lam': p['lru_lambda'][i][None],
                },
            })
        else:
            lp.update({
                'w_in': w_in_cd,
                'w_out': w_out_cd,
                'hgrn_norm': p['hgrn_norm'][i][None],
                'gla_norm': p['gla_norm'][i][None],
                'wg2': _pad_to(p['gla_wg2'][i], 0, LANES).astype(BF16),
                'bg': p['gla_bg'][i][None],
            })
        layers.append(lp)
    return layers


def _run_trunk(x, layers, lb_logits, st, cfg):
    bsz, s, d = x.shape
    m = bsz * s
    xf = x.reshape(m, d)
    no_heads = jnp.zeros((2, LANES), F32)
    kvs, convs, hs, shg, sgl, ffs = [], [], [], [], [], []
    for layer, lp in enumerate(layers):
        i = layer // 2
        if layer % 2 == 0:
            z = _norm_proj(xf, lp['norm_mix'], lp['w_in'], lp['head_gain'], layer=i, tm=cfg['tm'],
                           n_head_tiles=W_A // PROJ_TN)
            z3 = z.reshape(bsz, s, -1)
            if st is None:
                o_a = _attn_prompt(z3).reshape(m, W_A)
                keep = min(DILATED_GROUPS[-1][0], s)
                kvs.append(z3[:, s - keep:, W_A:3 * W_A].reshape(bsz, keep, 2, H_A, DH_A))
                conv_buf = jnp.zeros((bsz, CONV_B - 1, W_B), F32)
                h0 = jnp.zeros((bsz, W_B), F32)
            else:
                o_a = _attn_step(z3, st[0], layer=i).reshape(m, W_A)
                kvs.append(z3[:, :, W_A:3 * W_A].reshape(bsz, s, 2, H_A, DH_A))
                conv_buf, h0 = st[1][i], st[2][i]
            o_b, nconv, hlast = _rglru(z3, _tail_rows8(conv_buf), h0[:, None, :], lp['lru'], tt=cfg['tt'])
            convs.append(nconv[:, SUBLANES - (CONV_B - 1):])
            hs.append(hlast[:, 0])
            xf = _out_proj(o_a, o_b.reshape(m, W_B), lp['w_out'], xf, layer=i, tm=cfg['tm'], tn=PROJ_TN)
        else:
            z = _norm_proj(xf, lp['norm_mix'], lp['w_in'], no_heads, layer=i, tm=cfg['tm'], n_head_tiles=0)
            z3 = z.reshape(bsz, s, -1)
            if st is None:
                s_c0 = jnp.zeros((bsz, H_C, DK_C, DV_C), F32)
                s_d0 = jnp.zeros((bsz, H_D, DK_D, DV_D), F32)
            else:
                s_c0, s_d0 = st[3][i], st[4][i]
            o_c, n_c = _hgrn(z3, lb_logits, s_c0, lp['hgrn_norm'], layer_i=i, tt=cfg['tg'],
                             c=cfg['chunk'])
            o_d, n_d = _gla(z3, lp['wg2'], lp['bg'], s_d0, lp['gla_norm'], tt=cfg['tg'],
                            c=cfg['chunk'])
            shg.append(n_c)
            sgl.append(n_d)
            xf = _out_proj(o_c.reshape(m, -1), o_d.reshape(m, -1), lp['w_out'], xf, layer=i, tm=cfg['tm'],
                           tn=PROJ_TN)
        fbuf = jnp.zeros((bsz, CONV_F - 1, D_FF), F32) if st is None else st[5][layer]
        buf8 = _pad_to(_tail_rows8(fbuf), 2, D_FF_PAD)
        xf, nbuf = _ffn(xf, lp['norm_ffn'], lp['ffn'], buf8, layer=layer, tm=cfg['ffn_tm'], tf=FFN_TF,
                        nseg=cfg['ffn_nseg'], tiles_per_seq=cfg['ffn_tiles_per_seq'])
        ffs.append(nbuf[:, SUBLANES - (CONV_F - 1):, :D_FF])
    return (xf.reshape(bsz, s, d), jnp.stack(kvs), jnp.stack(convs), jnp.stack(hs), jnp.stack(shg),
            jnp.stack(sgl), jnp.stack(ffs))


def _config(bsz, s):
    m = bsz * s
    if s >= 512:
        return dict(tm=min(1024, m), tt=512, tg=256, chunk=CHUNK,
                    ffn_tm=512, ffn_nseg=1, ffn_tiles_per_seq=s // 512)
    chunk = min(CHUNK, s)
    return dict(tm=m, tt=s, tg=s, chunk=chunk,
                ffn_tm=m, ffn_nseg=bsz, ffn_tiles_per_seq=1)


def kernel(x_prompt, x_sample, cache_attn_kv, state_lru_conv, state_lru_h, state_hgrn, state_gla, state_ffn_conv, norm_mix, norm_ffn, w_in_ab, q_norm, k_norm, lru_conv_w, lru_conv_b, lru_wa, lru_ba, lru_wx, lru_bx, lru_lambda, w_out_ab, w_in_cd, hgrn_lb_logits, hgrn_norm, gla_wg2, gla_bg, gla_norm, w_out_cd, ffn_w_up, ffn_conv_w, ffn_conv_b, ffn_w_down):
    p = {
        'norm_mix': norm_mix, 'norm_ffn': norm_ffn, 'w_in_ab': w_in_ab, 'q_norm': q_norm, 'k_norm': k_norm,
        'lru_conv_w': lru_conv_w, 'lru_conv_b': lru_conv_b, 'lru_wa': lru_wa, 'lru_ba': lru_ba,
        'lru_wx': lru_wx, 'lru_bx': lru_bx, 'lru_lambda': lru_lambda, 'w_out_ab': w_out_ab,
        'w_in_cd': w_in_cd, 'hgrn_norm': hgrn_norm, 'gla_wg2': gla_wg2,
        'gla_bg': gla_bg, 'gla_norm': gla_norm, 'w_out_cd': w_out_cd, 'ffn_w_up': ffn_w_up,
        'ffn_conv_w': ffn_conv_w, 'ffn_conv_b': ffn_conv_b, 'ffn_w_down': ffn_w_down,
    }
    layers = _stage_params(p)
    y_p, p_kv, p_conv, p_h, p_hgrn, p_gla, p_ffn = _run_trunk(
        x_prompt, layers, hgrn_lb_logits, None, _config(*x_prompt.shape[:2]))
    st = (cache_attn_kv, state_lru_conv, state_lru_h, state_hgrn, state_gla, state_ffn_conv)
    y_s, s_kv, s_conv, s_h, s_hgrn, s_gla, s_ffn = _run_trunk(
        x_sample, layers, hgrn_lb_logits, st, _config(*x_sample.shape[:2]))
    return (y_p, y_s, p_kv, s_kv, p_conv, s_conv, p_h, s_h, p_hgrn, s_hgrn, p_gla, s_gla, p_ffn, s_ffn)
```

```python
import functools

import jax
import jax.numpy as jnp
from jax import lax
from jax.experimental import pallas as pl
from jax.experimental.pallas import tpu as pltpu

F32 = jnp.float32
BF16 = jnp.bfloat16
EPS = 1e-6
NEG_BIG = -1e30

LANES = 128
SUBLANES = 8
MXU_WIDTH = 256
VMEM_LIMIT = 56 * 1024 * 1024
FFN_VMEM_LIMIT = 60 * 1024 * 1024

H_A, DH_A = 8, 128
W_A = H_A * DH_A
DILATED_GROUPS = ((128, 1), (512, 4), (2048, 16))
ATTN_BLOCK = 128
W_B, H_B = 1024, 8
BH_B = W_B // H_B
CONV_B = 4
LRU_C = 8.0
LRU_BATCH = 2
H_C, DK_C, DV_C = 8, 128, 128
H_D, DK_D, DV_D = 4, 128, 256
GATE_RANK = 16
GLA_TAU = 16.0
CHUNK = 64
D_FF = 5504
D_FF_PAD = 5632
CONV_F = 3
CD_MAIN = 7168
CD_PAD = 7680
AB_TN = 1024
CD_TN = 1536
OUT_TN = 1024
FFN_TF = 512
FFN_TM = 1024


def _cparams(n_axes, vmem_limit=VMEM_LIMIT):
    return pltpu.CompilerParams(dimension_semantics=("arbitrary",) * n_axes,
                                vmem_limit_bytes=vmem_limit)


def _dot(a, b):
    return jnp.dot(a.astype(BF16), b.astype(BF16), preferred_element_type=F32)


def _dot_nt(a, b):
    return lax.dot_general(a.astype(BF16), b.astype(BF16), (((1,), (1,)), ((), ())),
                           preferred_element_type=F32)


def _dot_tn(a, b):
    return lax.dot_general(a.astype(BF16), b.astype(BF16), (((0,), (0,)), ((), ())),
                           preferred_element_type=F32)


def _rms(x, gain):
    ms = jnp.mean(x * x, axis=-1, keepdims=True)
    return (x * lax.rsqrt(ms + EPS)) * gain


def _softplus(x):
    return jnp.maximum(x, 0.0) + jnp.log1p(jnp.exp(-jnp.abs(x)))


def _silu(x):
    return x * jax.nn.sigmoid(x)


def _gelu_tanh(x):
    return 0.5 * x * (1.0 + jnp.tanh(0.7978845608028654 * (x + 0.044715 * (x * x * x))))


def _norm_proj_kernel(x_ref, g_ref, w_ref, hg_ref, o_ref, xn_ref, *, n_head_tiles, tn):
    j = pl.program_id(1)

    @pl.when(j == 0)
    def _():
        xn_ref[...] = _rms(x_ref[...], g_ref[...]).astype(BF16)

    acc = jnp.dot(xn_ref[...], w_ref[...], preferred_element_type=F32)
    if n_head_tiles == 0:
        o_ref[...] = acc
        return

    @pl.when(j >= 2 * n_head_tiles)
    def _():
        o_ref[...] = acc

    @pl.when(j < 2 * n_head_tiles)
    def _():
        gain = hg_ref[pl.ds(lax.div(j, n_head_tiles), 1), :]
        for h in range(tn // LANES):
            sl = slice(h * LANES, (h + 1) * LANES)
            o_ref[:, sl] = _rms(acc[:, sl], gain)


def _norm_proj(x, gain, w, head_gain, *, layer, tm, tn, n_head_tiles):
    m, d = x.shape
    n = w.shape[2]
    assert n % tn == 0
    kern = functools.partial(_norm_proj_kernel, n_head_tiles=n_head_tiles, tn=tn)
    return pl.pallas_call(
        kern,
        grid=(m // tm, n // tn),
        in_specs=[
            pl.BlockSpec((tm, d), lambda i, j: (i, 0)),
            pl.BlockSpec((1, d), lambda i, j: (0, 0)),
            pl.BlockSpec((None, d, tn), lambda i, j: (layer, 0, j)),
            pl.BlockSpec(head_gain.shape, lambda i, j: (0, 0)),
        ],
        out_specs=pl.BlockSpec((tm, tn), lambda i, j: (i, j)),
        out_shape=jax.ShapeDtypeStruct((m, n), F32),
        scratch_shapes=[pltpu.VMEM((tm, d), BF16)],
        compiler_params=_cparams(2),
        name="norm_proj",
    )(x, gain, w, head_gain)


ATTN_MAX_STRIDE = 4


def _ds(start, size, stride):
    return pl.ds(start, size, stride=stride) if stride > 1 else pl.ds(start, size)


def _attn_group(load, store, d, span, n_keys, first_span):
    qb = ATTN_BLOCK
    row = lax.broadcasted_iota(jnp.int32, (qb, 2 * qb), 0)
    col = lax.broadcasted_iota(jnp.int32, (qb, 2 * qb), 1)
    scale = DH_A ** -0.5
    band = (col >= row + (qb - n_keys)) & (col <= row + qb)
    band_first = band & (jnp.logical_not(first_span) | (col >= qb))
    blk = qb * d
    for sb in range(span // blk):
        valid = band_first if sb == 0 else band
        for r in range(d):
            q0 = sb * blk + r
            q = load('q', q0, qb, d)
            if sb == 0:
                k = jnp.concatenate([load('kp', span - blk + r, qb, d), load('kc', r, qb, d)], axis=0)
                v = jnp.concatenate([load('vp', span - blk + r, qb, d), load('vc', r, qb, d)], axis=0)
            else:
                k = load('kc', q0 - blk, 2 * qb, d)
                v = load('vc', q0 - blk, 2 * qb, d)
            s = jnp.where(valid, _dot_nt(q, k) * scale, NEG_BIG)
            m = jnp.max(s, axis=-1, keepdims=True)
            p = jnp.where(valid, jnp.exp(s - m), 0.0)
            den = jnp.sum(p, axis=-1, keepdims=True)
            store(q0, qb, d, _dot(p, v) / den, jnp.broadcast_to(m + jnp.log(den), (qb, DH_A)))


def _attn_span_kernel(q_ref, kp_ref, kc_ref, vp_ref, vc_ref, o_ref, oacc_ref, lacc_ref, stage_ref, so_ref,
                      *, span):
    first_span = pl.program_id(1) == 0
    names = ('q', 'kp', 'kc', 'vp', 'vc')
    srcs = dict(zip(names, (q_ref, kp_ref, kc_ref, vp_ref, vc_ref)))
    for g, (window, d) in enumerate(DILATED_GROUPS):
        n_keys = window // d
        if d <= ATTN_MAX_STRIDE:
            def load(name, start, size, stride):
                return srcs[name][0, _ds(start, size, stride), :]

            def store(start, size, stride, o, lse, g=g):
                oacc_ref[g, _ds(start, size, stride), :] = o
                lacc_ref[g, _ds(start, size, stride), :] = lse

            _attn_group(load, store, d, span, n_keys, first_span)
            continue
        d1 = ATTN_MAX_STRIDE
        sub = span // d1
        for r1 in range(d1):
            for a, name in enumerate(names):
                stage_ref[a] = srcs[name][0, _ds(r1, sub, d1), :]

            def load(name, start, size, stride):
                return stage_ref[names.index(name), _ds(start, size, stride), :]

            def store(start, size, stride, o, lse):
                so_ref[0, _ds(start, size, stride), :] = o
                so_ref[1, _ds(start, size, stride), :] = lse

            _attn_group(load, store, d // d1, sub, n_keys, first_span)
            oacc_ref[g, _ds(r1, sub, d1), :] = so_ref[0]
            lacc_ref[g, _ds(r1, sub, d1), :] = so_ref[1]
    la, lb, lc = lacc_ref[0], lacc_ref[1], lacc_ref[2]
    m = jnp.maximum(jnp.maximum(la, lb), lc)
    ea, eb, ec = jnp.exp(la - m), jnp.exp(lb - m), jnp.exp(lc - m)
    den = ea + eb + ec
    o_ref[0] = ((ea / den) * oacc_ref[0] + (eb / den) * oacc_ref[1] + (ec / den) * oacc_ref[2]).astype(o_ref.dtype)


def _attn_prompt(z):
    bsz, s, _ = z.shape
    span = max(d for _, d in DILATED_GROUPS) * ATTN_BLOCK
    assert s % span == 0 and len(DILATED_GROUPS) == 3
    assert all(w // d <= ATTN_BLOCK and span % (d * ATTN_BLOCK) == 0 for w, d in DILATED_GROUPS)
    blk = (1, span, DH_A)

    def cur(c):
        return pl.BlockSpec(blk, lambda b, n, h: (b, n, c * H_A + h))

    def prev(c):
        return pl.BlockSpec(blk, lambda b, n, h: (b, jnp.maximum(n - 1, 0), c * H_A + h))

    return pl.pallas_call(
        functools.partial(_attn_span_kernel, span=span),
        grid=(bsz, s // span, H_A),
        in_specs=[cur(0), prev(1), cur(1), prev(2), cur(2)],
        out_specs=pl.BlockSpec(blk, lambda b, n, h: (b, n, h)),
        out_shape=jax.ShapeDtypeStruct((bsz, s, W_A), BF16),
        scratch_shapes=[pltpu.VMEM((3, span, DH_A), F32), pltpu.VMEM((3, span, DH_A), F32),
                        pltpu.VMEM((5, span // ATTN_MAX_STRIDE, DH_A), F32),
                        pltpu.VMEM((2, span // ATTN_MAX_STRIDE, DH_A), F32)],
        compiler_params=_cparams(3),
        name="attn_prompt",
    )(z, z, z, z, z)


def _attn_step_kernel(z_ref, kc_ref, vc_ref, o_ref, *, cache_len, t_new):
    scale = DH_A ** -0.5
    nq = H_A * t_new

    def heads(c0):
        return jnp.concatenate([z_ref[0, :, c0 + h * DH_A:c0 + (h + 1) * DH_A] for h in range(H_A)], axis=0)

    q, kn, vn = heads(0), heads(W_A), heads(2 * W_A)
    sn = _dot_nt(q, kn) * scale
    log_t, log_h = t_new.bit_length() - 1, H_A.bit_length() - 1
    rn = lax.broadcasted_iota(jnp.int32, (nq, nq), 0)
    cn = lax.broadcasted_iota(jnp.int32, (nq, nq), 1)
    head_n = (rn >> log_t) == (cn >> log_t)
    dn = (rn & (t_new - 1)) - (cn & (t_new - 1))
    outs, lses = [], []
    for window, d in DILATED_GROUPS:
        assert d & (d - 1) == 0
        rows = min(window, cache_len)
        nc = rows * H_A
        kc = kc_ref[0, cache_len - rows:cache_len].reshape(nc, DH_A)
        vc = vc_ref[0, cache_len - rows:cache_len].reshape(nc, DH_A)
        sc = _dot_nt(q, kc) * scale
        rc = lax.broadcasted_iota(jnp.int32, (nq, nc), 0)
        cc = lax.broadcasted_iota(jnp.int32, (nq, nc), 1)
        dc = rows + (rc & (t_new - 1)) - (cc >> log_h)
        okc = ((rc >> log_t) == (cc & (H_A - 1))) & (dc <= window) & ((dc & (d - 1)) == 0)
        okn = head_n & (dn >= 0) & (dn <= window) & ((dn & (d - 1)) == 0)
        m = jnp.maximum(jnp.max(jnp.where(okc, sc, NEG_BIG), axis=-1, keepdims=True),
                        jnp.max(jnp.where(okn, sn, NEG_BIG), axis=-1, keepdims=True))
        pc = jnp.where(okc, jnp.exp(sc - m), 0.0)
        pn = jnp.where(okn, jnp.exp(sn - m), 0.0)
        den = jnp.sum(pc, axis=-1, keepdims=True) + jnp.sum(pn, axis=-1, keepdims=True)
        outs.append((_dot(pc, vc) + _dot(pn, vn)) / den)
        lses.append(m + jnp.log(den))
    m = jnp.maximum(jnp.maximum(lses[0], lses[1]), lses[2])
    es = [jnp.exp(l - m) for l in lses]
    den = es[0] + es[1] + es[2]
    o = (es[0] / den) * outs[0] + (es[1] / den) * outs[1] + (es[2] / den) * outs[2]
    for h in range(H_A):
        o_ref[0, :, h * DH_A:(h + 1) * DH_A] = o[h * t_new:(h + 1) * t_new]


def _attn_step(z, cache, *, layer):
    bsz, t_new, zc = z.shape
    cache_len = cache.shape[2]
    assert t_new == SUBLANES and H_A & (H_A - 1) == 0
    old = (None, 1, cache_len, None, H_A, DH_A)
    return pl.pallas_call(
        functools.partial(_attn_step_kernel, cache_len=cache_len, t_new=t_new),
        grid=(bsz,),
        in_specs=[
            pl.BlockSpec((1, t_new, zc), lambda b: (b, 0, 0)),
            pl.BlockSpec(old, lambda b: (layer, b, 0, 0, 0, 0)),
            pl.BlockSpec(old, lambda b: (layer, b, 0, 1, 0, 0)),
        ],
        out_specs=pl.BlockSpec((1, t_new, W_A), lambda b: (b, 0, 0)),
        out_shape=jax.ShapeDtypeStruct((bsz, t_new, W_A), F32),
        compiler_params=_cparams(1),
        name="attn_step",
    )(z, cache, cache)


def _rglru_kernel(xb_ref, gb_ref, cbuf_ref, h0_ref, cw_ref, cb_ref, wa_ref, ba_ref, wx_ref, bx_ref,
                  lam_ref, ob_ref, nconv_ref, hlast_ref, xp_ref, a_ref, u_ref, hs_ref, hc_ref, *, tt):
    i = pl.program_id(1)
    halo = SUBLANES
    nb = xb_ref.shape[0]

    @pl.when(i == 0)
    def _():
        for bi in range(nb):
            xp_ref[bi, 0:halo, :] = cbuf_ref[bi]
            hc_ref[bi] = h0_ref[bi]

    @pl.when(i > 0)
    def _():
        for bi in range(nb):
            xp_ref[bi, 0:halo, :] = xp_ref[bi, tt:tt + halo, :]

    for bi in range(nb):
        xp_ref[bi, halo:halo + tt, :] = xb_ref[bi]
        nconv_ref[bi] = xp_ref[bi, tt:tt + halo, :]

        xc = cb_ref[...]
        for j in range(CONV_B):
            off = halo - (CONV_B - 1) + j
            xc = xc + cw_ref[j:j + 1, :] * xp_ref[bi, off:off + tt, :]

        for h in range(H_B):
            sl = slice(h * BH_B, (h + 1) * BH_B)
            xh = xc[:, sl]
            r = 0.5 * jnp.tanh(0.5 * (_dot(xh, wa_ref[h]) + ba_ref[:, sl])) + 0.5
            ig = 0.5 * jnp.tanh(0.5 * (_dot(xh, wx_ref[h]) + bx_ref[:, sl])) + 0.5
            log_a = -LRU_C * r * _softplus(-lam_ref[:, sl])
            a = jnp.exp(log_a)
            a_ref[bi, :, sl] = a
            u_ref[bi, :, sl] = jnp.sqrt(-jnp.tanh(log_a) * (a * a + 1.0)) * (ig * xh)

    def block(kb, hs):
        base = pl.multiple_of(kb * SUBLANES, SUBLANES)
        out = []
        for bi in range(nb):
            h = hs[bi]
            a8 = a_ref[bi, pl.ds(base, SUBLANES), :]
            u8 = u_ref[bi, pl.ds(base, SUBLANES), :]
            rows = []
            for r in range(SUBLANES):
                h = a8[r:r + 1] * h + u8[r:r + 1]
                rows.append(h)
            hs_ref[bi, pl.ds(base, SUBLANES), :] = jnp.concatenate(rows, axis=0)
            out.append(h)
        return tuple(out)

    hs = lax.fori_loop(0, tt // SUBLANES, block, tuple(hc_ref[bi] for bi in range(nb)))
    for bi in range(nb):
        hc_ref[bi] = hs[bi]
        hlast_ref[bi] = hs[bi]
        ob_ref[bi] = (hs_ref[bi] * _gelu_tanh(gb_ref[bi])).astype(ob_ref.dtype)


def _rglru(z, conv_buf8, h0, p, *, tt):
    bsz, s, _ = z.shape
    nb = LRU_BATCH
    tile = pl.BlockSpec((nb, tt, W_B), lambda b, i: (b, i, 3))
    gate = pl.BlockSpec((nb, tt, W_B), lambda b, i: (b, i, 4))
    per_b8 = pl.BlockSpec((nb, SUBLANES, W_B), lambda b, i: (b, 0, 0))
    per_b1 = pl.BlockSpec((nb, 1, W_B), lambda b, i: (b, 0, 0))

    def whole(a):
        return pl.BlockSpec(a.shape, lambda b, i: (0,) * a.ndim)

    weights = [p['cw'], p['cb'], p['wa'], p['ba'], p['wx'], p['bx'], p['lam']]
    return pl.pallas_call(
        functools.partial(_rglru_kernel, tt=tt),
        grid=(bsz // nb, s // tt),
        in_specs=[tile, gate, per_b8, per_b1] + [whole(a) for a in weights],
        out_specs=[pl.BlockSpec((nb, tt, W_B), lambda b, i: (b, i, 0)), per_b8, per_b1],
        out_shape=[jax.ShapeDtypeStruct((bsz, s, W_B), BF16),
                   jax.ShapeDtypeStruct((bsz, SUBLANES, W_B), F32),
                   jax.ShapeDtypeStruct((bsz, 1, W_B), F32)],
        scratch_shapes=[pltpu.VMEM((nb, tt + SUBLANES, W_B), F32), pltpu.VMEM((nb, tt, W_B), F32),
                        pltpu.VMEM((nb, tt, W_B), F32), pltpu.VMEM((nb, tt, W_B), F32),
                        pltpu.VMEM((nb, 1, W_B), F32)],
        compiler_params=_cparams(2),
        name="rglru",
    )(z, z, conv_buf8, h0, *weights)


GLA_PACK = 4
GLA_BATCH = 2


def _pair_masks(c):
    wide = GLA_PACK * c
    t = lax.broadcasted_iota(jnp.int32, (c, wide), 0)
    col = lax.broadcasted_iota(jnp.int32, (c, wide), 1)
    s = col & (c - 1)
    pairs = [t == s]
    m = c // 2
    while m >= 1:
        same_block = (t & -(2 * m)) == (s & -(2 * m))
        pairs.append(same_block & ((t & m) != 0) & ((s & m) == 0))
        m //= 2
    log_c = c.bit_length() - 1
    return [p & ((col >> log_c) == (j % GLA_PACK)) for j, p in enumerate(pairs)]


def _cumsum_rows(tri, g):
    hi = g.astype(BF16)
    r1 = g - hi.astype(F32)
    mid = r1.astype(BF16)
    lo = (r1 - mid.astype(F32)).astype(BF16)
    return (jnp.dot(tri, hi, preferred_element_type=F32) + jnp.dot(tri, mid, preferred_element_type=F32)
            + jnp.dot(tri, lo, preferred_element_type=F32))


def _level_ref(b, m):
    c, kdim = b.shape
    size = 2 * m
    if m >= SUBLANES:
        parts = [jnp.broadcast_to(b[j * size + m - 1:j * size + m], (size, kdim)) for j in range(c // size)]
        return parts[0] if len(parts) == 1 else jnp.concatenate(parts, axis=0)
    b3 = b.reshape(c // SUBLANES, SUBLANES, kdim)
    sub = lax.broadcasted_iota(jnp.int32, b3.shape, 1)
    ref = None
    for r0 in range(m - 1, SUBLANES, size):
        cand = jnp.broadcast_to(b3[:, r0:r0 + 1, :], b3.shape)
        ref = cand if ref is None else jnp.where(sub >= r0 - (m - 1), cand, ref)
    return ref.reshape(c, kdim)


def _gla_chunk(q, k, v, b, st, masks):
    c = q.shape[0]
    o = _dot_nt(q * jnp.exp(b), st)
    scaled = [(q, k)]
    m = c // 2
    while m >= 1:
        e = jnp.exp(-jnp.abs(b - _level_ref(b, m)))
        scaled.append((q * e, k * e))
        m //= 2
    acc = None
    for j0 in range(0, len(scaled), GLA_PACK):
        grp = scaled[j0:j0 + GLA_PACK]
        grp = grp + [scaled[0]] * (GLA_PACK - len(grp))
        p = _dot_nt(jnp.concatenate([qs for qs, _ in grp], axis=0),
                    jnp.concatenate([ks for _, ks in grp], axis=0))
        for j in range(j0, min(j0 + GLA_PACK, len(scaled))):
            blk = jnp.where(masks[j], p[(j - j0) * c:(j - j0 + 1) * c], 0.0)
            acc = blk if acc is None else acc + blk
    o = o + _dot(acc, jnp.concatenate([v] * GLA_PACK, axis=0))
    blast = b[c - 1:c]
    kd = k * jnp.exp(blast - b)
    st_new = st * jnp.exp(blast) + _dot_tn(v, kd)
    return o, st_new


def _hgrn_kernel(z_ref, lbl_ref, s0_ref, ng_ref, o_ref, sout_ref, st_ref, *, layer_i, c, tt):
    i = pl.program_id(1)
    nb = z_ref.shape[0]

    @pl.when(i == 0)
    def _():
        for bi in range(nb):
            for h in range(H_C):
                st_ref[bi * H_C + h] = s0_ref[bi, h].T

    lg = lbl_ref[...]
    e = jnp.exp(lg - jnp.max(lg, axis=0, keepdims=True))
    pr = e / jnp.sum(e, axis=0, keepdims=True)
    acc = pr[0:1]
    for l in range(1, layer_i + 1):
        acc = acc + pr[l:l + 1]
    lb = acc - pr[0:1]
    tri = (lax.broadcasted_iota(jnp.int32, (c, c), 0) >= lax.broadcasted_iota(jnp.int32, (c, c), 1)).astype(BF16)
    masks = _pair_masks(c)
    ng = ng_ref[...]
    key_w = H_C * DK_C

    def chunk(ci, carry):
        r0 = pl.multiple_of(ci * c, c)
        rs = pl.ds(r0, c)
        for bi in range(nb):
            zf_all = z_ref[bi, rs, key_w:2 * key_w]
            b_all = _cumsum_rows(tri, jnp.log(lb + (1.0 - lb) * jax.nn.sigmoid(zf_all)))
            for h in range(H_C):
                ks = slice(h * DK_C, (h + 1) * DK_C)
                q = z_ref[bi, rs, ks]
                zf = z_ref[bi, rs, key_w + h * DK_C:key_w + (h + 1) * DK_C]
                v = z_ref[bi, rs, 2 * key_w + h * DV_C:2 * key_w + (h + 1) * DV_C]
                gate = z_ref[bi, rs, 2 * key_w + H_C * DV_C + h * DV_C:2 * key_w + H_C * DV_C + (h + 1) * DV_C]
                k = (1.0 - lb[:, ks]) * jax.nn.sigmoid(-zf)
                o, st_new = _gla_chunk(q, k, v, b_all[:, ks], st_ref[bi * H_C + h], masks)
                st_ref[bi * H_C + h] = st_new
                o_ref[bi, rs, h * DV_C:(h + 1) * DV_C] = (_rms(o, ng) * _silu(gate)).astype(o_ref.dtype)
        return carry

    lax.fori_loop(0, tt // c, chunk, 0)

    @pl.when(i == pl.num_programs(1) - 1)
    def _():
        for bi in range(nb):
            for h in range(H_C):
                sout_ref[bi, h] = st_ref[bi * H_C + h].T


def _hgrn(z, lb_logits, s0, norm_gain, *, layer_i, tt, c):
    bsz, s, _ = z.shape
    nb = GLA_BATCH
    state = pl.BlockSpec((nb, H_C, DK_C, DV_C), lambda b, i: (b, 0, 0, 0))
    width = 2 * H_C * DK_C + 2 * H_C * DV_C
    return pl.pallas_call(
        functools.partial(_hgrn_kernel, layer_i=layer_i, c=c, tt=tt),
        grid=(bsz // nb, s // tt),
        in_specs=[pl.BlockSpec((nb, tt, width), lambda b, i: (b, i, 0)),
                  pl.BlockSpec(lb_logits.shape, lambda b, i: (0, 0)),
                  state,
                  pl.BlockSpec((1, DV_C), lambda b, i: (0, 0))],
        out_specs=[pl.BlockSpec((nb, tt, H_C * DV_C), lambda b, i: (b, i, 0)), state],
        out_shape=[jax.ShapeDtypeStruct((bsz, s, H_C * DV_C), BF16),
                   jax.ShapeDtypeStruct(s0.shape, F32)],
        scratch_shapes=[pltpu.VMEM((nb * H_C, DV_C, DK_C), F32)],
        compiler_params=_cparams(2),
        name="hgrn2",
    )(z, lb_logits, s0, norm_gain)


def _gla_kernel(qk_ref, v_ref, r_ref, ad_ref, wg_ref, bg_ref, s0_ref, ng_ref, o_ref, sout_ref, st_ref,
                *, c, tt):
    i = pl.program_id(1)
    nb = qk_ref.shape[0]

    @pl.when(i == 0)
    def _():
        for bi in range(nb):
            for h in range(H_D):
                st_ref[bi * H_D + h] = s0_ref[bi, h].T

    tri = (lax.broadcasted_iota(jnp.int32, (c, c), 0) >= lax.broadcasted_iota(jnp.int32, (c, c), 1)).astype(BF16)
    masks = _pair_masks(c)
    ng = ng_ref[...]
    key_w = H_D * DK_D
    scale = DK_D ** -0.5

    def chunk(ci, carry):
        r0 = pl.multiple_of(ci * c, c)
        rs = pl.ds(r0, c)
        for bi in range(nb):
            za = _dot(ad_ref[bi, rs, 0:LANES], wg_ref[...]) + bg_ref[...]
            b_all = _cumsum_rows(tri, -_softplus(-za) * (1.0 / GLA_TAU))
            for h in range(H_D):
                ks = slice(h * DK_D, (h + 1) * DK_D)
                vs = slice(h * DV_D, (h + 1) * DV_D)
                q = qk_ref[bi, rs, ks] * scale
                k = qk_ref[bi, rs, key_w + h * DK_D:key_w + (h + 1) * DK_D]
                v = v_ref[bi, rs, vs]
                o, st_new = _gla_chunk(q, k, v, b_all[:, ks], st_ref[bi * H_D + h], masks)
                st_ref[bi * H_D + h] = st_new
                o_ref[bi, rs, vs] = (_rms(o, ng) * _silu(r_ref[bi, rs, vs])).astype(o_ref.dtype)
        return carry

    lax.fori_loop(0, tt // c, chunk, 0)

    @pl.when(i == pl.num_programs(1) - 1)
    def _():
        for bi in range(nb):
            for h in range(H_D):
                sout_ref[bi, h] = st_ref[bi * H_D + h].T


def _gla(z, wg2p, bg, s0, norm_gain, *, tt, c):
    bsz, s, _ = z.shape
    w = H_D * DV_D
    nb = GLA_BATCH
    state = pl.BlockSpec((nb, H_D, DK_D, DV_D), lambda b, i: (b, 0, 0, 0))

    def col(cb):
        return pl.BlockSpec((nb, tt, w), lambda b, i: (b, i, cb))

    return pl.pallas_call(
        functools.partial(_gla_kernel, c=c, tt=tt),
        grid=(bsz // nb, s // tt),
        in_specs=[col(4), col(5), col(6),
                  pl.BlockSpec((nb, tt, CD_PAD - CD_MAIN), lambda b, i: (b, i, CD_MAIN // (CD_PAD - CD_MAIN))),
                  pl.BlockSpec(wg2p.shape, lambda b, i: (0, 0)),
                  pl.BlockSpec(bg.shape, lambda b, i: (0, 0)),
                  state,
                  pl.BlockSpec((1, DV_D), lambda b, i: (0, 0))],
        out_specs=[pl.BlockSpec((nb, tt, w), lambda b, i: (b, i, 0)), state],
        out_shape=[jax.ShapeDtypeStruct((bsz, s, w), BF16),
                   jax.ShapeDtypeStruct(s0.shape, F32)],
        scratch_shapes=[pltpu.VMEM((nb * H_D, DV_D, DK_D), F32)],
        compiler_params=_cparams(2),
        name="gla",
    )(z, z, z, z, wg2p, bg, s0, norm_gain)


def _out_proj_kernel(a1_ref, a2_ref, w1_ref, w2_ref, x_ref, o_ref):
    acc = _dot(a1_ref[...], w1_ref[...]) + _dot(a2_ref[...], w2_ref[...])
    o_ref[...] = x_ref[...] + acc


def _out_proj(a1, a2, w, x, *, layer, tm, tn):
    m, k1 = a1.shape
    k2 = a2.shape[1]
    assert k1 == k2 and w.shape[1] == k1 + k2
    n = w.shape[2]
    return pl.pallas_call(
        _out_proj_kernel,
        grid=(m // tm, n // tn),
        in_specs=[pl.BlockSpec((tm, k1), lambda i, j: (i, 0)),
                  pl.BlockSpec((tm, k2), lambda i, j: (i, 0)),
                  pl.BlockSpec((None, k1, tn), lambda i, j: (layer, 0, j)),
                  pl.BlockSpec((None, k2, tn), lambda i, j: (layer, 1, j)),
                  pl.BlockSpec((tm, tn), lambda i, j: (i, j))],
        out_specs=pl.BlockSpec((tm, tn), lambda i, j: (i, j)),
        out_shape=jax.ShapeDtypeStruct((m, n), F32),
        compiler_params=_cparams(2),
        name="out_proj",
    )(a1, a2, w, w, x)


def _ffn_kernel(x_ref, gain_ref, wg_ref, wu_ref, cw_ref, cb_ref, wd_ref, buf_ref, o_ref, nbuf_ref,
                xn_ref, gp_ref, carry_ref, *, nseg, seg, tiles_per_seq, chain):
    i = pl.program_id(0)
    j = pl.program_id(1)
    halo = SUBLANES

    @pl.when(j == 0)
    def _():
        x = x_ref[...]
        xn_ref[...] = _rms(x, gain_ref[...]).astype(BF16)
        o_ref[...] = x

    xn = xn_ref[...]
    first = lax.rem(i, tiles_per_seq) == 0
    down = None
    for c in range(wg_ref.shape[1] // chain):
        cs = slice(c * chain, (c + 1) * chain)
        g = jnp.dot(xn, wg_ref[:, cs], preferred_element_type=F32)
        u = jnp.dot(xn, wu_ref[:, cs], preferred_element_type=F32)
        acts = []
        for s in range(nseg):
            gs = g[s * seg:(s + 1) * seg]
            gp_ref[s, 0:halo, cs] = jnp.where(first, buf_ref[s, :, cs], carry_ref[j, s, :, cs])
            gp_ref[s, halo:halo + seg, cs] = gs
            carry_ref[j, s, :, cs] = gs[seg - halo:seg]
            nbuf_ref[s, :, cs] = gs[seg - halo:seg]
            gc = cb_ref[:, cs]
            for t in range(CONV_F):
                off = halo - (CONV_F - 1) + t
                gc = gc + cw_ref[t:t + 1, cs] * gp_ref[s, off:off + seg, cs]
            acts.append(_silu(gc) * u[s * seg:(s + 1) * seg])
        act = acts[0] if nseg == 1 else jnp.concatenate(acts, axis=0)
        d = _dot(act, wd_ref[cs, :])
        down = d if down is None else down + d
    o_ref[...] += down


def _ffn(x, gain, p, buf8, *, layer, tm, tf, nseg, tiles_per_seq):
    m, d = x.shape
    seg = tm // nseg
    nj = D_FF_PAD // tf
    kern = functools.partial(_ffn_kernel, nseg=nseg, seg=seg, tiles_per_seq=tiles_per_seq, chain=tf)
    state = pl.BlockSpec((nseg, SUBLANES, tf), lambda i, j: (lax.div(i, tiles_per_seq), 0, j))
    tail = pl.BlockSpec((nseg, SUBLANES, tf), lambda i, j: (i, 0, j))
    n_tiles = m // tm
    tails = pl.pallas_call(
        kern,
        grid=(n_tiles, nj),
        in_specs=[pl.BlockSpec((tm, d), lambda i, j: (i, 0)),
                  pl.BlockSpec((1, d), lambda i, j: (0, 0)),
                  pl.BlockSpec((None, d, tf), lambda i, j: (layer, 0, j)),
                  pl.BlockSpec((None, d, tf), lambda i, j: (layer, 0, j)),
                  pl.BlockSpec((None, SUBLANES, tf), lambda i, j: (layer, 0, j)),
                  pl.BlockSpec((None, 1, tf), lambda i, j: (layer, 0, j)),
                  pl.BlockSpec((None, tf, d), lambda i, j: (layer, j, 0)),
                  state],
        out_specs=[pl.BlockSpec((tm, d), lambda i, j: (i, 0)), tail],
        out_shape=[jax.ShapeDtypeStruct((m, d), F32),
                   jax.ShapeDtypeStruct((n_tiles * nseg, SUBLANES, D_FF_PAD), F32)],
        scratch_shapes=[pltpu.VMEM((tm, d), BF16),
                        pltpu.VMEM((nseg, seg + SUBLANES, tf), F32),
                        pltpu.VMEM((nj, nseg, SUBLANES, tf), F32)],
        compiler_params=_cparams(2, FFN_VMEM_LIMIT),
        name="conv_ffn",
    )(x, gain, p['wg'], p['wu'], p['cw'], p['cb'], p['wd'], buf8)
    y, nbuf = tails
    nbuf = nbuf.reshape(n_tiles // tiles_per_seq, tiles_per_seq, nseg, SUBLANES, D_FF_PAD)[:, -1]
    return y, nbuf.reshape(-1, SUBLANES, D_FF_PAD)


def _pad_to(a, axis, size):
    pad = [(0, 0)] * a.ndim
    pad[axis] = (0, size - a.shape[axis])
    return jnp.pad(a, pad)


def _tail_rows8(buf):
    return jnp.pad(buf, ((0, 0), (SUBLANES - buf.shape[1], 0), (0, 0)))


def _cast_pad_cols_kernel(*refs, n_in, ncols):
    for src, dst in zip(refs[:n_in], refs[n_in:]):
        dst[:, :ncols] = src[...].astype(BF16)
        dst[:, ncols:] = jnp.zeros((dst.shape[0], dst.shape[1] - ncols), BF16)


def _cast_pad_cols(w, col_blocks, ncols, padded, *, tr):
    layers, rows, _ = w.shape
    n_in = len(col_blocks)
    assert ncols % LANES == 0 and padded % LANES == 0
    return pl.pallas_call(
        functools.partial(_cast_pad_cols_kernel, n_in=n_in, ncols=ncols),
        grid=(layers, rows // tr),
        in_specs=[pl.BlockSpec((None, tr, ncols), functools.partial(lambda cb, l, i: (l, i, cb), cb))
                  for cb in col_blocks],
        out_specs=[pl.BlockSpec((None, tr, padded), lambda l, i: (l, i, 0))] * n_in,
        out_shape=[jax.ShapeDtypeStruct((layers, rows, padded), BF16)] * n_in,
        compiler_params=_cparams(2),
        name="stage_cols",
    )(*([w] * n_in))


def _stage_params(p):
    depth = p['norm_mix'].shape[0]
    wg, wu = _cast_pad_cols(p['ffn_w_up'], (0, 1), D_FF, D_FF_PAD, tr=256)
    ffn = {
        'wg': wg,
        'wu': wu,
        'cw': _pad_to(_pad_to(p['ffn_conv_w'], 2, D_FF_PAD), 1, SUBLANES),
        'cb': _pad_to(p['ffn_conv_b'][:, None, :], 2, D_FF_PAD),
        'wd': _pad_to(p['ffn_w_down'].astype(BF16), 1, D_FF_PAD),
    }
    w_in_ab = p['w_in_ab'].astype(BF16)
    w_out_ab = p['w_out_ab'].astype(BF16)
    w_in_cd = _pad_to(p['w_in_cd'].astype(BF16), 2, CD_PAD)
    w_out_cd = p['w_out_cd'].astype(BF16)
    layers = []
    for layer in range(depth):
        i = layer // 2
        lp = {
            'norm_mix': p['norm_mix'][layer][None],
            'norm_ffn': p['norm_ffn'][layer][None],
            'ffn': ffn,
        }
        if layer % 2 == 0:
            lp.update({
                'w_in': w_in_ab,
                'head_gain': jnp.stack([p['q_norm'][i], p['k_norm'][i]]),
                'w_out': w_out_ab,
                'lru': {
                    'cw': _pad_to(p['lru_conv_w'][i], 0, SUBLANES),
                    'cb': p['lru_conv_b'][i][None],
                    'wa': p['lru_wa'][i].astype(BF16),
                    'ba': p['lru_ba'][i][None],
                    'wx': p['lru_wx'][i].astype(BF16),
                    'bx': p['lru_bx'][i][None],
                    'lam': p['lru_lambda'][i][None],
                },
            })
        else:
            lp.update({
                'w_in': w_in_cd,
                'w_out': w_out_cd,
                'hgrn_norm': p['hgrn_norm'][i][None],
                'gla_norm': p['gla_norm'][i][None],
                'wg2': _pad_to(p['gla_wg2'][i], 0, LANES).astype(BF16),
                'bg': p['gla_bg'][i][None],
            })
        layers.append(lp)
    return layers


def _run_trunk(x, layers, lb_logits, st, cfg):
    bsz, s, d = x.shape
    m = bsz * s
    xf = x.reshape(m, d)
    no_heads = jnp.zeros((2, LANES), F32)
    kvs, convs, hs, shg, sgl, ffs = [], [], [], [], [], []
    for layer, lp in enumerate(layers):
        i = layer // 2
        if layer % 2 == 0:
            z = _norm_proj(xf, lp['norm_mix'], lp['w_in'], lp['head_gain'], layer=i, tm=cfg['tm'], tn=AB_TN,
                           n_head_tiles=W_A // AB_TN)
            z3 = z.reshape(bsz, s, -1)
            if st is None:
                o_a = _attn_prompt(z3).reshape(m, W_A)
                keep = min(DILATED_GROUPS[-1][0], s)
                kvs.append(z3[:, s - keep:, W_A:3 * W_A].reshape(bsz, keep, 2, H_A, DH_A))
                conv_buf = jnp.zeros((bsz, CONV_B - 1, W_B), F32)
                h0 = jnp.zeros((bsz, W_B), F32)
            else:
                o_a = _attn_step(z3, st[0], layer=i).reshape(m, W_A)
                kvs.append(z3[:, :, W_A:3 * W_A].reshape(bsz, s, 2, H_A, DH_A))
                conv_buf, h0 = st[1][i], st[2][i]
            o_b, nconv, hlast = _rglru(z3, _tail_rows8(conv_buf), h0[:, None, :], lp['lru'], tt=cfg['tt'])
            convs.append(nconv[:, SUBLANES - (CONV_B - 1):])
            hs.append(hlast[:, 0])
            xf = _out_proj(o_a, o_b.reshape(m, W_B), lp['w_out'], xf, layer=i, tm=cfg['tm'], tn=OUT_TN)
        else:
            z = _norm_proj(xf, lp['norm_mix'], lp['w_in'], no_heads, layer=i, tm=cfg['tm'], tn=CD_TN,
                           n_head_tiles=0)
            z3 = z.reshape(bsz, s, -1)
            if st is None:
                s_c0 = jnp.zeros((bsz, H_C, DK_C, DV_C), F32)
                s_d0 = jnp.zeros((bsz, H_D, DK_D, DV_D), F32)
            else:
                s_c0, s_d0 = st[3][i], st[4][i]
            o_c, n_c = _hgrn(z3, lb_logits, s_c0, lp['hgrn_norm'], layer_i=i, tt=cfg['tg'],
                             c=cfg['chunk'])
            o_d, n_d = _gla(z3, lp['wg2'], lp['bg'], s_d0, lp['gla_norm'], tt=cfg['tg'],
                            c=cfg['chunk'])
            shg.append(n_c)
            sgl.append(n_d)
            xf = _out_proj(o_c.reshape(m, -1), o_d.reshape(m, -1), lp['w_out'], xf, layer=i, tm=cfg['tm'],
                           tn=OUT_TN)
        fbuf = jnp.zeros((bsz, CONV_F - 1, D_FF), F32) if st is None else st[5][layer]
        buf8 = _pad_to(_tail_rows8(fbuf), 2, D_FF_PAD)
        xf, nbuf = _ffn(xf, lp['norm_ffn'], lp['ffn'], buf8, layer=layer, tm=cfg['ffn_tm'], tf=FFN_TF,
                        nseg=cfg['ffn_nseg'], tiles_per_seq=cfg['ffn_tiles_per_seq'])
        ffs.append(nbuf[:, SUBLANES - (CONV_F - 1):, :D_FF])
    return (xf.reshape(bsz, s, d), jnp.stack(kvs), jnp.stack(convs), jnp.stack(hs), jnp.stack(shg),
            jnp.stack(sgl), jnp.stack(ffs))


def _config(bsz, s):
    m = bsz * s
    if s >= 512:
        return dict(tm=min(1024, m), tt=512, tg=512, chunk=CHUNK,
                    ffn_tm=FFN_TM, ffn_nseg=1, ffn_tiles_per_seq=s // FFN_TM)
    chunk = min(CHUNK, s)
    return dict(tm=m, tt=s, tg=s, chunk=chunk,
                ffn_tm=m, ffn_nseg=bsz, ffn_tiles_per_seq=1)


def kernel(x_prompt, x_sample, cache_attn_kv, state_lru_conv, state_lru_h, state_hgrn, state_gla, state_ffn_conv, norm_mix, norm_ffn, w_in_ab, q_norm, k_norm, lru_conv_w, lru_conv_b, lru_wa, lru_ba, lru_wx, lru_bx, lru_lambda, w_out_ab, w_in_cd, hgrn_lb_logits, hgrn_norm, gla_wg2, gla_bg, gla_norm, w_out_cd, ffn_w_up, ffn_conv_w, ffn_conv_b, ffn_w_down):
    p = {
        'norm_mix': norm_mix, 'norm_ffn': norm_ffn, 'w_in_ab': w_in_ab, 'q_norm': q_norm, 'k_norm': k_norm,
        'lru_conv_w': lru_conv_w, 'lru_conv_b': lru_conv_b, 'lru_wa': lru_wa, 'lru_ba': lru_ba,
        'lru_wx': lru_wx, 'lru_bx': lru_bx, 'lru_lambda': lru_lambda, 'w_out_ab': w_out_ab,
        'w_in_cd': w_in_cd, 'hgrn_norm': hgrn_norm, 'gla_wg2': gla_wg2,
        'gla_bg': gla_bg, 'gla_norm': gla_norm, 'w_out_cd': w_out_cd, 'ffn_w_up': ffn_w_up,
        'ffn_conv_w': ffn_conv_w, 'ffn_conv_b': ffn_conv_b, 'ffn_w_down': ffn_w_down,
    }
    layers = _stage_params(p)
    y_p, p_kv, p_conv, p_h, p_hgrn, p_gla, p_ffn = _run_trunk(
        x_prompt, layers, hgrn_lb_logits, None, _config(*x_prompt.shape[:2]))
    st = (cache_attn_kv, state_lru_conv, state_lru_h, state_hgrn, state_gla, state_ffn_conv)
    y_s, s_kv, s_conv, s_h, s_hgrn, s_gla, s_ffn = _run_trunk(
        x_sample, layers, hgrn_lb_logits, st, _config(*x_sample.shape[:2]))
    return (y_p, y_s, p_kv, s_kv, p_conv, s_conv, p_h, s_h, p_hgrn, s_hgrn, p_gla, s_gla, p_ffn, s_ffn)
```
